```python
import math
import jax, jax.numpy as jnp
from jax import lax
import numpy as np

D_MODEL = 1024
BATCH = 8
SEQ = 4096
DEPTH = 2
DEC_BATCH = 128
DEC_SEQ = 8
PAST_LEN = 16384
PAGE_SIZE = 128

N_META = 16
MLA_HEADS = 8
MLA_Q_RANK = 256
MLA_KV_RANK = 128
MLA_NOPE = 64
MLA_ROPE = 32
MLA_V = 64
MLA_SCALE = (MLA_NOPE + MLA_ROPE) ** -0.5
ROPE_THETA = 10000.0
Q_BLOCK = 128
GLA_HEADS = 4
GLA_DK = 32
GLA_DV = 64
GLA_GATE_RANK = 16
GLA_TAU = 16.0
ML_HEADS = 4
ML_DH = 64
ML_WIDTH = ML_HEADS * ML_DH
CONV_W = 4
CHUNK = 64
MIX_WIDTH = MLA_HEADS * MLA_V + GLA_HEADS * GLA_DV + ML_WIDTH
N_EXPERTS = 16
N_GROUPS = 4
E_PER_GROUP = N_EXPERTS // N_GROUPS
TOP_K = 2
D_EXPERT = 256
ALPHA = (2 * DEPTH) ** 0.25
BETA = (8 * DEPTH) ** -0.25
LN_EPS = 1e-5
RMS_EPS = 1e-6
IN_SIZES = (MLA_Q_RANK, MLA_KV_RANK, MLA_ROPE,
            GLA_HEADS * GLA_DK, GLA_HEADS * GLA_DK, GLA_HEADS * GLA_DV, GLA_GATE_RANK, GLA_HEADS * GLA_DV,
            ML_WIDTH, ML_WIDTH, ML_HEADS, ML_HEADS, ML_WIDTH)
D_IN = sum(IN_SIZES)

kernel_name = 'hybrid_mla_gla_mlstm_moe_step'


def layer_norm(x, g, b):
    xf = x.astype(jnp.float32)
    mu = jnp.mean(xf, -1, keepdims=True)
    var = jnp.mean(jnp.square(xf - mu), -1, keepdims=True)
    return ((xf - mu) * lax.rsqrt(var + LN_EPS) * g + b).astype(x.dtype)


def rms_norm(x, g):
    xf = x.astype(jnp.float32)
    return (xf * lax.rsqrt(jnp.mean(xf * xf, -1, keepdims=True) + RMS_EPS) * g).astype(x.dtype)


def head_norm(x):
    xf = x.astype(jnp.float32)
    mu = jnp.mean(xf, -1, keepdims=True)
    var = jnp.mean(jnp.square(xf - mu), -1, keepdims=True)
    return (xf - mu) * lax.rsqrt(var + LN_EPS)


def rope(x, pos):
    half = x.shape[-1] // 2
    inv = ROPE_THETA ** (-jnp.arange(half, dtype=jnp.float32) / half)
    ang = pos.astype(jnp.float32)[:, None] * inv
    ang = ang.reshape(ang.shape[:1] + (1,) * (x.ndim - 3) + (half,))
    cos, sin = jnp.cos(ang), jnp.sin(ang)
    xf = x.astype(jnp.float32)
    x1, x2 = xf[..., :half], xf[..., half:]
    return jnp.concatenate([x1 * cos - x2 * sin, x2 * cos + x1 * sin], -1).astype(x.dtype)


def mla_scores(q_lat, q_rope, c, kr):
    s = jnp.einsum('bqhr,bkr->bhqk', q_lat, c) + jnp.einsum('bqhe,bke->bhqk', q_rope, kr)
    return s.astype(jnp.float32) * MLA_SCALE


def mla_prompt_attend(q_lat, q_rope, c, kr):
    B, L = c.shape[:2]
    nb = -(-L // Q_BLOCK)
    pad = nb * Q_BLOCK - L

    def blocks(a):
        a = jnp.pad(a, ((0, 0), (0, pad), (0, 0), (0, 0)))
        return jnp.moveaxis(a.reshape((B, nb, Q_BLOCK) + a.shape[2:]), 1, 0)

    kpos = jnp.arange(L)

    def one_block(args):
        ql, qr, i = args
        s = mla_scores(ql, qr, c, kr)
        qpos = i * Q_BLOCK + jnp.arange(Q_BLOCK)
        s = jnp.where(kpos[None, :] <= qpos[:, None], s, -jnp.inf)
        p = jax.nn.softmax(s, axis=-1).astype(c.dtype)
        return jnp.einsum('bhqk,bkr->bqhr', p, c)

    o = lax.map(one_block, (blocks(q_lat), blocks(q_rope), jnp.arange(nb)))
    o = jnp.moveaxis(o, 0, 1).reshape(B, nb * Q_BLOCK, MLA_HEADS, MLA_KV_RANK)
    return o[:, :L]


def mla_sample_attend(q_lat, q_rope, c_new, kr_new, c_past, kr_past):
    T = c_new.shape[1]
    s_past = mla_scores(q_lat, q_rope, c_past, kr_past)
    s_new = mla_scores(q_lat, q_rope, c_new, kr_new)
    s_new = jnp.where(jnp.tril(jnp.ones((T, T), bool)), s_new, -jnp.inf)
    m = jnp.maximum(s_past.max(-1, keepdims=True), s_new.max(-1, keepdims=True))
    e_past = jnp.exp(s_past - m)
    e_new = jnp.exp(s_new - m)
    z = e_past.sum(-1, keepdims=True) + e_new.sum(-1, keepdims=True)
    return (jnp.einsum('bhqk,bkr->bqhr', (e_past / z).astype(c_past.dtype), c_past)
            + jnp.einsum('bhqk,bkr->bqhr', (e_new / z).astype(c_new.dtype), c_new))


def gla_chunk(S, q, k, v, la):
    C = q.shape[1]
    b = jnp.cumsum(la, axis=1)
    causal = jnp.tril(jnp.ones((C, C), bool))[None, :, :, None, None]
    diff = jnp.where(causal, b[:, :, None] - b[:, None, :], -jnp.inf)
    A = jnp.einsum('bthk,bshk,btshk->bhts', q, k, jnp.exp(diff))
    intra = jnp.einsum('bhts,bshv->bthv', A, v)
    inter = jnp.einsum('bthk,bhkv->bthv', q * jnp.exp(b), S)
    bl = b[:, -1]
    S = jnp.exp(bl)[..., None] * S + jnp.einsum('bshk,bshv->bhkv', k * jnp.exp(bl[:, None] - b), v)
    return S, intra + inter


def mlstm_chunk(state, q, k, v, ig, lf):
    Cs, ns, ms = state
    C = q.shape[1]
    F = jnp.cumsum(lf, axis=1)
    a = F + ms[:, None]
    causal = jnp.tril(jnp.ones((C, C), bool))[None, :, :, None]
    d = jnp.where(causal, F[:, :, None] - F[:, None, :] + ig[:, None], -jnp.inf)
    m = jnp.maximum(a, d.max(axis=2))
    wp = jnp.exp(a - m)
    W = jnp.exp(d - m[:, :, None])
    qk = jnp.einsum('bthd,bshd->btsh', q, k) * W
    num = wp[..., None] * jnp.einsum('bthk,bhkv->bthv', q, Cs) + jnp.einsum('btsh,bshv->bthv', qk, v)
    den = wp * jnp.einsum('bthk,bhk->bth', q, ns) + qk.sum(2)
    h = num / jnp.maximum(jnp.abs(den), jnp.exp(-m))[..., None]
    m_last = m[:, -1]
    wl = jnp.exp(F[:, -1] + ms - m_last)
    ws = jnp.exp(F[:, -1][:, None] - F + ig - m_last[:, None])
    Cs = wl[..., None, None] * Cs + jnp.einsum('bsh,bshk,bshv->bhkv', ws, k, v)
    ns = wl[..., None] * ns + jnp.einsum('bsh,bshk->bhk', ws, k)
    return (Cs, ns, m_last), h


def _scan_chunks(chunk_fn, state, inputs, c):
    B, T = inputs[0].shape[:2]
    n = T // c
    xs = [jnp.moveaxis(a.reshape((B, n, c) + a.shape[2:]), 1, 0) for a in inputs]
    state, ys = lax.scan(lambda s, xi: chunk_fn(s, *xi), state, xs)
    ys = jnp.moveaxis(ys, 0, 1)
    return state, ys.reshape((B, T) + ys.shape[3:])


def run_chunks(chunk_fn, state, inputs, lead, chunk):
    outs = []
    if lead:
        state, y = _scan_chunks(chunk_fn, state, [a[:, :lead] for a in inputs], lead)
        outs.append(y)
        inputs = [a[:, lead:] for a in inputs]
    state, y = _scan_chunks(chunk_fn, state, inputs, chunk)
    outs.append(y)
    return state, jnp.concatenate(outs, axis=1)


def mixer_block(h, pos, lead, chunk, lw, past):
    (w_in, mla_q_norm, mla_w_uq, mla_kv_norm, mla_w_ukv, gla_w_a, gla_b_a, gla_norm,
     ml_conv_w, ml_conv_b, ml_w_q, ml_w_k, ml_b_i, ml_b_f, ml_norm, ml_skip, w_out) = lw
    B, T, _ = h.shape
    f32 = jnp.float32
    (cq, ckv, kr, gq, gk, gv, ga, gr, mu, mv, mi, mf, mo) = jnp.split(
        h @ w_in, np.cumsum(IN_SIZES)[:-1].tolist(), axis=-1)

    q = (rms_norm(cq, mla_q_norm) @ mla_w_uq).reshape(B, T, MLA_HEADS, MLA_NOPE + MLA_ROPE)
    q_nope, q_rope = q[..., :MLA_NOPE], rope(q[..., MLA_NOPE:], pos)
    c_kv = rms_norm(ckv, mla_kv_norm)
    k_rope = rope(kr, pos)
    w_ukv = mla_w_ukv.reshape(MLA_KV_RANK, MLA_HEADS, MLA_NOPE + MLA_V)
    q_lat = jnp.einsum('bthn,rhn->bthr', q_nope, w_ukv[..., :MLA_NOPE])
    if past is None:
        o_lat = mla_prompt_attend(q_lat, q_rope, c_kv, k_rope)
    else:
        o_lat = mla_sample_attend(q_lat, q_rope, c_kv, k_rope, past[0], past[1])
    o_mla = jnp.einsum('bthr,rhv->bthv', o_lat, w_ukv[..., MLA_NOPE:]).reshape(B, T, -1)

    la = (jax.nn.log_sigmoid((ga @ gla_w_a + gla_b_a).astype(f32)) / GLA_TAU).reshape(B, T, GLA_HEADS, GLA_DK)
    gla_in = (gq.reshape(B, T, GLA_HEADS, GLA_DK).astype(f32) * GLA_DK ** -0.5,
              gk.reshape(B, T, GLA_HEADS, GLA_DK).astype(f32),
              gv.reshape(B, T, GLA_HEADS, GLA_DV).astype(f32), la)
    S0 = jnp.zeros((B, GLA_HEADS, GLA_DK, GLA_DV), f32) if past is None else past[2].astype(f32)
    S_new, o = run_chunks(gla_chunk, S0, gla_in, lead, chunk)
    o_gla = (rms_norm(o, gla_norm) * jax.nn.silu(gr.reshape(B, T, GLA_HEADS, GLA_DV).astype(f32)))
    o_gla = o_gla.reshape(B, T, -1).astype(h.dtype)

    buf = jnp.zeros((B, CONV_W - 1, ML_WIDTH), mu.dtype) if past is None else past[6].astype(mu.dtype)
    up = jnp.concatenate([buf, mu], axis=1)
    uc = ml_conv_b
    for i in range(CONV_W):
        uc = uc + up[:, i:i + T] * ml_conv_w[i]
    uc = jax.nn.silu(uc)
    uch = uc.reshape(B, T, ML_HEADS, ML_DH)
    mq = jnp.einsum('bthd,hde->bthe', uch, ml_w_q).astype(f32)
    mk = (jnp.einsum('bthd,hde->bthe', uch, ml_w_k) * ML_DH ** -0.5).astype(f32)
    mvv = mv.reshape(B, T, ML_HEADS, ML_DH).astype(f32)
    ig = (mi + ml_b_i).astype(f32)
    lf = jax.nn.log_sigmoid((mf + ml_b_f).astype(f32))
    if past is None:
        st0 = (jnp.zeros((B, ML_HEADS, ML_DH, ML_DH), f32), jnp.zeros((B, ML_HEADS, ML_DH), f32),
               jnp.zeros((B, ML_HEADS), f32))
    else:
        st0 = (past[3].astype(f32), past[4].astype(f32), past[5].astype(f32))
    (Cs, ns, ms), hc = run_chunks(mlstm_chunk, st0, (mq, mk, mvv, ig, lf), lead, chunk)
    hc = hc * jax.nn.sigmoid(mo.astype(f32)).reshape(B, T, ML_HEADS, ML_DH)
    o_ml = (head_norm(hc) * ml_norm.reshape(ML_HEADS, ML_DH)).reshape(B, T, -1) + ml_skip * uc

    y = jnp.concatenate([o_mla, o_gla, o_ml.astype(h.dtype)], axis=-1) @ w_out
    return y, (c_kv, k_rope, S_new, Cs, ns, ms, up[:, -(CONV_W - 1):])


def moe(x, router_w, router_bias, w_gate, w_up, w_down):
    B, T, D = x.shape
    xt = x.reshape(-1, D)
    scores = jax.nn.sigmoid((xt @ router_w).astype(jnp.float32))
    sel = (scores + router_bias.astype(jnp.float32)).reshape(-1, N_GROUPS, E_PER_GROUP)
    g = jnp.argmax(lax.top_k(sel, TOP_K)[0].sum(-1), axis=-1)
    gmask = jnp.arange(N_GROUPS)[None, :] == g[:, None]
    masked = jnp.where(gmask[:, :, None], sel, -jnp.inf).reshape(-1, N_EXPERTS)
    _, idx = lax.top_k(masked, TOP_K)
    gate = jnp.take_along_axis(scores, idx, axis=-1)
    gate = gate / gate.sum(-1, keepdims=True)
    w = (jax.nn.one_hot(idx, N_EXPERTS, dtype=jnp.float32) * gate[..., None]).sum(1)
    hg = jnp.einsum('nd,edf->nef', xt, w_gate)
    hu = jnp.einsum('nd,edf->nef', xt, w_up)
    hh = jax.nn.silu(hg) * hu * w[:, :, None].astype(x.dtype)
    return jnp.einsum('nef,efd->nd', hh, w_down).reshape(B, T, D)


def trunk(x, pos, lead, chunk, past_fn, layer_w, norm_w, moe_w, router):
    ln1_g, ln1_b, ln2_g, ln2_b = norm_w
    states = []
    for l in range(DEPTH):
        past = None if past_fn is None else past_fn(l)
        y, st = mixer_block(x, pos, lead, chunk, tuple(a[l] for a in layer_w), past)
        x = layer_norm(ALPHA * x + y, ln1_g[l], ln1_b[l])
        x = layer_norm(ALPHA * x + moe(x, router[0], router[1], *(a[l] for a in moe_w)), ln2_g[l], ln2_b[l])
        states.append(st)
    return x, tuple(jnp.stack(s) for s in zip(*states))


def setup_inputs(seed: int = 0) -> dict:
    key = jax.random.key(seed)
    ks = list(jax.random.split(key, 48))

    def nrm(shape, scale=1.0):
        return jax.random.normal(ks.pop(), shape, jnp.float32) * scale

    n_pages = PAST_LEN // PAGE_SIZE
    n_pool = (DEC_BATCH * n_pages * 5) // 4
    page_table = jax.random.permutation(ks.pop(), n_pool)[:DEC_BATCH * n_pages]
    page_table = page_table.reshape(DEC_BATCH, n_pages).astype(jnp.int32)
    return {
        'x_prompt': nrm((BATCH, SEQ, D_MODEL)),
        'x_sample': nrm((DEC_BATCH, DEC_SEQ, D_MODEL)),
        'cache_kv_latent': nrm((DEPTH, n_pool, PAGE_SIZE, MLA_KV_RANK)),
        'cache_k_rope': nrm((DEPTH, n_pool, PAGE_SIZE, MLA_ROPE)),
        'state_gla': nrm((DEPTH, DEC_BATCH, GLA_HEADS, GLA_DK, GLA_DV)),
        'state_mlstm_C': nrm((DEPTH, DEC_BATCH, ML_HEADS, ML_DH, ML_DH)),
        'state_mlstm_n': nrm((DEPTH, DEC_BATCH, ML_HEADS, ML_DH)),
        'state_mlstm_m': nrm((DEPTH, DEC_BATCH, ML_HEADS)),
        'state_mlstm_conv': nrm((DEPTH, DEC_BATCH, CONV_W - 1, ML_WIDTH)),
        'page_table': page_table,
        'meta_tokens': nrm((N_META, D_MODEL)),
        'ln_in_g': 1.0 + nrm((D_MODEL,), 0.02),
        'ln_in_b': nrm((D_MODEL,), 0.02),
        'w_in': nrm((DEPTH, D_MODEL, D_IN), D_MODEL ** -0.5),
        'mla_q_norm': 1.0 + nrm((DEPTH, MLA_Q_RANK), 0.02),
        'mla_w_uq': nrm((DEPTH, MLA_Q_RANK, MLA_HEADS * (MLA_NOPE + MLA_ROPE)), MLA_Q_RANK ** -0.5),
        'mla_kv_norm': 1.0 + nrm((DEPTH, MLA_KV_RANK), 0.02),
        'mla_w_ukv': nrm((DEPTH, MLA_KV_RANK, MLA_HEADS * (MLA_NOPE + MLA_V)), MLA_KV_RANK ** -0.5),
        'gla_w_a': nrm((DEPTH, GLA_GATE_RANK, GLA_HEADS * GLA_DK), GLA_GATE_RANK ** -0.5),
        'gla_b_a': nrm((DEPTH, GLA_HEADS * GLA_DK), 0.1),
        'gla_norm': 1.0 + nrm((DEPTH, GLA_DV), 0.02),
        'ml_conv_w': nrm((DEPTH, CONV_W, ML_WIDTH), CONV_W ** -0.5),
        'ml_conv_b': nrm((DEPTH, ML_WIDTH), 0.02),
        'ml_w_q': nrm((DEPTH, ML_HEADS, ML_DH, ML_DH), ML_DH ** -0.5),
        'ml_w_k': nrm((DEPTH, ML_HEADS, ML_DH, ML_DH), ML_DH ** -0.5),
        'ml_b_i': nrm((DEPTH, ML_HEADS), 0.1),
        'ml_b_f': jnp.linspace(3.0, 6.0, ML_HEADS, dtype=jnp.float32)[None] + nrm((DEPTH, ML_HEADS), 0.1),
        'ml_norm': 1.0 + nrm((DEPTH, ML_WIDTH), 0.02),
        'ml_skip': 1.0 + nrm((DEPTH, ML_WIDTH), 0.02),
        'w_out': nrm((DEPTH, MIX_WIDTH, D_MODEL), MIX_WIDTH ** -0.5 * BETA),
        'ln1_g': 1.0 + nrm((DEPTH, D_MODEL), 0.02),
        'ln1_b': nrm((DEPTH, D_MODEL), 0.02),
        'ln2_g': 1.0 + nrm((DEPTH, D_MODEL), 0.02),
        'ln2_b': nrm((DEPTH, D_MODEL), 0.02),
        'router_w': nrm((D_MODEL, N_EXPERTS), D_MODEL ** -0.5),
        'router_bias': nrm((N_EXPERTS,), 0.01),
        'moe_w_gate': nrm((DEPTH, N_EXPERTS, D_MODEL, D_EXPERT), D_MODEL ** -0.5),
        'moe_w_up': nrm((DEPTH, N_EXPERTS, D_MODEL, D_EXPERT), D_MODEL ** -0.5),
        'moe_w_down': nrm((DEPTH, N_EXPERTS, D_EXPERT, D_MODEL), D_EXPERT ** -0.5 * BETA),
    }


def reference(x_prompt, x_sample, cache_kv_latent, cache_k_rope, state_gla, state_mlstm_C, state_mlstm_n,
              state_mlstm_m, state_mlstm_conv, page_table, meta_tokens, ln_in_g, ln_in_b, w_in, mla_q_norm,
              mla_w_uq, mla_kv_norm, mla_w_ukv, gla_w_a, gla_b_a, gla_norm, ml_conv_w, ml_conv_b, ml_w_q, ml_w_k,
              ml_b_i, ml_b_f, ml_norm, ml_skip, w_out, ln1_g, ln1_b, ln2_g, ln2_b, router_w, router_bias,
              moe_w_gate, moe_w_up, moe_w_down):
    layer_w = (w_in, mla_q_norm, mla_w_uq, mla_kv_norm, mla_w_ukv, gla_w_a, gla_b_a, gla_norm,
               ml_conv_w, ml_conv_b, ml_w_q, ml_w_k, ml_b_i, ml_b_f, ml_norm, ml_skip, w_out)
    norm_w = (ln1_g, ln1_b, ln2_g, ln2_b)
    moe_w = (moe_w_gate, moe_w_up, moe_w_down)
    router = (router_w, router_bias)

    B, S, D = x_prompt.shape
    meta = jnp.broadcast_to(meta_tokens[None].astype(x_prompt.dtype), (B, N_META, D))
    xp = layer_norm(jnp.concatenate([meta, x_prompt], axis=1), ln_in_g, ln_in_b)
    pos_p = jnp.arange(S + N_META)
    hp, (p_kv, p_kr, p_gla, p_C, p_n, p_m, p_conv) = trunk(
        xp, pos_p, N_META, CHUNK, None, layer_w, norm_w, moe_w, router)
    y_prompt = hp[:, N_META:]

    n_seq, T = x_sample.shape[:2]

    def sample_past(l):
        c_past = cache_kv_latent[l, page_table].reshape(n_seq, -1, MLA_KV_RANK)
        kr_past = cache_k_rope[l, page_table].reshape(n_seq, -1, MLA_ROPE)
        return (c_past, kr_past, state_gla[l], state_mlstm_C[l], state_mlstm_n[l], state_mlstm_m[l],
                state_mlstm_conv[l])

    xs = layer_norm(x_sample, ln_in_g, ln_in_b)
    pos_s = PAST_LEN + jnp.arange(T)
    y_sample, (s_kv, s_kr, s_gla, s_C, s_n, s_m, s_conv) = trunk(
        xs, pos_s, 0, T, sample_past, layer_w, norm_w, moe_w, router)

    return (y_prompt, y_sample, p_kv, p_kr, p_gla, p_C, p_n, p_m, p_conv,
            s_kv, s_kr, s_gla, s_C, s_n, s_m, s_conv)
```

```python
import functools

import numpy as np
import jax
import jax.numpy as jnp
from jax import lax
from jax.experimental import pallas as pl
from jax.experimental.pallas import tpu as pltpu

F32 = jnp.float32
BF16 = jnp.bfloat16

N_META = 16
MLA_HEADS = 8
MLA_Q_RANK = 256
MLA_KV_RANK = 128
MLA_NOPE = 64
MLA_ROPE = 32
MLA_V = 64
MLA_SCALE = (MLA_NOPE + MLA_ROPE) ** -0.5
ROPE_THETA = 10000.0
GLA_HEADS = 4
GLA_DK = 32
GLA_DV = 64
GLA_GATE_RANK = 16
GLA_TAU = 16.0
ML_HEADS = 4
ML_DH = 64
ML_WIDTH = ML_HEADS * ML_DH
CONV_W = 4
N_EXPERTS = 16
N_GROUPS = 4
E_PER_GROUP = N_EXPERTS // N_GROUPS
D_EXPERT = 256
DEPTH = 2
ALPHA = (2 * DEPTH) ** 0.25
LN_EPS = 1e-5
RMS_EPS = 1e-6
IN_SIZES = (MLA_Q_RANK, MLA_KV_RANK, MLA_ROPE,
            GLA_HEADS * GLA_DK, GLA_HEADS * GLA_DK, GLA_HEADS * GLA_DV, GLA_GATE_RANK, GLA_HEADS * GLA_DV,
            ML_WIDTH, ML_WIDTH, ML_HEADS, ML_HEADS, ML_WIDTH)

LANE = 128
CHUNK = 128
PAD = 128
D_PROJ = 2048
PJ_CQ, PJ_GV, PJ_GR, PJ_MU, PJ_MV, PJ_MO = 0, 1, 2, 3, 4, 5
PJ_CKV, PJ_GQ, PJ_GK, PJ_MISC = 12, 13, 14, 15
MS_KR, MS_KRSW, MS_GA, MS_MI, MS_MF = 0, 32, 64, 80, 84
NEG_BIG = -1e30
VMEM_LIMIT = 56 * 1024 * 1024

NT_DIMS = (((1,), (1,)), ((), ()))
TN_DIMS = (((0,), (0,)), ((), ()))


def _dot(a, b):
    return jnp.dot(a, b, preferred_element_type=F32)


def _dot_nt(a, b):
    return lax.dot_general(a, b, NT_DIMS, preferred_element_type=F32)


def _dot_tn(a, b):
    return lax.dot_general(a, b, TN_DIMS, preferred_element_type=F32)


NN_DIMS = (((1,), (0,)), ((), ()))


def _mm(a, b, dims, precise):
    ah, bh = a.astype(BF16), b.astype(BF16)
    out = lax.dot_general(ah, bh, dims, preferred_element_type=F32)
    if precise:
        am = (a - ah.astype(F32)).astype(BF16)
        bm = (b - bh.astype(F32)).astype(BF16)
        out = (out + lax.dot_general(am, bh, dims, preferred_element_type=F32)
               + lax.dot_general(ah, bm, dims, preferred_element_type=F32))
    return out


def _split3(x):
    hi = x.astype(BF16)
    r = x - hi.astype(F32)
    mid = r.astype(BF16)
    lo = (r - mid.astype(F32)).astype(BF16)
    return hi, mid, lo


def _dot_exact_lhs(m01, x):
    hi, mid, lo = _split3(x)
    return _dot(m01, hi) + _dot(m01, mid) + _dot(m01, lo)


def _layer_norm(x, g, b):
    mu = jnp.mean(x, axis=-1, keepdims=True)
    xc = x - mu
    var = jnp.mean(xc * xc, axis=-1, keepdims=True)
    return xc * lax.rsqrt(var + LN_EPS) * g + b


def _params(sem):
    return pltpu.CompilerParams(dimension_semantics=sem, vmem_limit_bytes=VMEM_LIMIT)


def _const_spec(shape):
    nd = len(shape)
    return pl.BlockSpec(shape, lambda *_: (0,) * nd)


def _fold_kernel(a_ref, b_ref, o_ref):
    o_ref[0] = jnp.dot(a_ref[0], b_ref[0], preferred_element_type=F32, precision=lax.Precision.HIGHEST)


def _fold(a, b):
    h, m, k = a.shape
    n = b.shape[2]
    return pl.pallas_call(
        _fold_kernel,
        grid=(h,),
        in_specs=[pl.BlockSpec((1, m, k), lambda i: (i, 0, 0)), pl.BlockSpec((1, k, n), lambda i: (i, 0, 0))],
        out_specs=pl.BlockSpec((1, m, n), lambda i: (i, 0, 0)),
        out_shape=jax.ShapeDtypeStruct((h, m, n), F32),
        compiler_params=_params(("arbitrary",)),
        name="fold",
    )(a, b)


def _in_proj_kernel(x_ref, g_ref, b_ref, w_ref, proj_ref, *xn_refs, ln_in):
    x = x_ref[...]
    if ln_in:
        x = _layer_norm(x, g_ref[...], b_ref[...])
        xn_refs[0][...] = x
    xb = x.astype(BF16)
    if w_ref.shape[0] == 3 * x.shape[1]:
        x_mid = (x - xb.astype(F32)).astype(BF16)
        xb = jnp.concatenate([xb, x_mid, xb], axis=1)
    proj_ref[...] = _dot(xb, w_ref[...])


def _split2_bits(w):
    bits = lax.bitcast_convert_type(w.astype(F32), jnp.uint32) & jnp.uint32(0xFFFF0000)
    hi = lax.bitcast_convert_type(bits, F32)
    return hi.astype(BF16), (w - hi).astype(BF16)


def _split_weight(w):
    hi, mid = _split2_bits(w)
    return jnp.concatenate([hi, hi, mid], axis=0)


def _in_proj(x, g, b, w, ln_in, tm):
    n, d = x.shape
    out_shape = [jax.ShapeDtypeStruct((n, D_PROJ), F32)]
    out_specs = [pl.BlockSpec((tm, D_PROJ), lambda i: (i, 0))]
    if ln_in:
        out_shape.append(jax.ShapeDtypeStruct((n, d), F32))
        out_specs.append(pl.BlockSpec((tm, d), lambda i: (i, 0)))
    res = pl.pallas_call(
        functools.partial(_in_proj_kernel, ln_in=ln_in),
        grid=(n // tm,),
        in_specs=[pl.BlockSpec((tm, d), lambda i: (i, 0)), _const_spec((1, d)), _const_spec((1, d)),
                  pl.BlockSpec(w.shape, lambda i: (0, 0), pipeline_mode=pl.Buffered(1))],
        out_specs=out_specs,
        out_shape=out_shape,
        compiler_params=_params(("parallel",)),
        name="in_proj",
    )(x, g, b, w)
    return (res[0], res[1]) if ln_in else (res[0], x)


def _mla_prep_kernel(cq_ref, ckv_ref, misc_ref, cq_tab, sq_tab, k_tab, qn_ref, kvn_ref, wq_ref,
                     qlat_ref, qrope_ref, ckvn_ref, krope_ref, kcat_ref, vt_ref):
    cq = cq_ref[...]
    cqn = cq * lax.rsqrt(jnp.mean(cq * cq, axis=-1, keepdims=True) + RMS_EPS) * qn_ref[...]
    q = _dot(cqn.astype(BF16), wq_ref[...])
    nl = MLA_HEADS * MLA_KV_RANK
    nr = MLA_HEADS * MLA_ROPE
    qlat_ref[...] = (q[:, :nl] * MLA_SCALE).astype(BF16)
    rot = q[:, nl:nl + nr] * cq_tab[...] + q[:, nl + nr:] * sq_tab[...]
    qrope_ref[...] = (rot * MLA_SCALE).astype(BF16)
    ckv = ckv_ref[...]
    ckvn = ckv * lax.rsqrt(jnp.mean(ckv * ckv, axis=-1, keepdims=True) + RMS_EPS) * kvn_ref[...]
    ckvn_ref[...] = ckvn
    p = misc_ref[...] * k_tab[...]
    kr = p[:, MS_KR:MS_KR + MLA_ROPE] + p[:, MS_KRSW:MS_KRSW + MLA_ROPE]
    krope_ref[...] = kr
    kcat_ref[...] = jnp.concatenate([ckvn, kr], axis=1).astype(BF16)
    vt_ref[0] = ckvn.T.astype(BF16)


def _mla_prep(proj, tabs, qn, kvn, wq, tp, tab_blocks):
    n = proj.shape[0]
    cq_tab, sq_tab, k_tab = tabs
    nq = wq.shape[1]
    dk = MLA_KV_RANK + MLA_ROPE
    row = lambda c: (lambda i: (i, c))
    tab = lambda i: (i % tab_blocks, 0)
    return pl.pallas_call(
        _mla_prep_kernel,
        grid=(n // tp,),
        in_specs=[pl.BlockSpec((tp, 256), row(PJ_CQ)), pl.BlockSpec((tp, LANE), row(PJ_CKV)),
                  pl.BlockSpec((tp, LANE), row(PJ_MISC)),
                  pl.BlockSpec((tp, 256), tab), pl.BlockSpec((tp, 256), tab), pl.BlockSpec((tp, LANE), tab),
                  _const_spec((1, MLA_Q_RANK)), _const_spec((1, MLA_KV_RANK)), _const_spec((MLA_Q_RANK, nq))],
        out_specs=[pl.BlockSpec((tp, 1024), row(0)), pl.BlockSpec((tp, 256), row(0)),
                   pl.BlockSpec((tp, MLA_KV_RANK), row(0)), pl.BlockSpec((tp, MLA_ROPE), row(0)),
                   pl.BlockSpec((tp, dk), row(0)), pl.BlockSpec((1, MLA_KV_RANK, tp), lambda i: (i, 0, 0))],
        out_shape=[jax.ShapeDtypeStruct((n, 1024), BF16), jax.ShapeDtypeStruct((n, 256), BF16),
                   jax.ShapeDtypeStruct((n, MLA_KV_RANK), F32), jax.ShapeDtypeStruct((n, MLA_ROPE), F32),
                   jax.ShapeDtypeStruct((n, dk), BF16), jax.ShapeDtypeStruct((n // tp, MLA_KV_RANK, tp), BF16)],
        compiler_params=_params(("parallel",)),
        name="mla_prep",
    )(proj, proj, proj, cq_tab, sq_tab, k_tab, qn, kvn, wq)


def _attn_prompt_kernel(ql_ref, qr_ref, k_ref, vt_ref, o_ref, m_ref, l_ref, acc_ref, *, tq, tk, lo_key):
    qi = pl.program_id(1)
    m_rows = tq * MLA_HEADS
    ql = ql_ref[...]
    qr = qr_ref[...]
    q = jnp.concatenate(
        [jnp.concatenate([ql[:, h * MLA_KV_RANK:(h + 1) * MLA_KV_RANK], qr[:, h * MLA_ROPE:(h + 1) * MLA_ROPE]], axis=1)
         for h in range(MLA_HEADS)], axis=0)
    qpos = qi * tq + lax.broadcasted_iota(jnp.int32, (1, m_rows), 1) % tq
    lowk = jnp.where(qpos < lo_key, 0, lo_key)

    def block(j, masked, first):
        start = j * tk if isinstance(j, int) else pl.multiple_of(j * tk, tk)
        kb = k_ref[pl.ds(start, tk), :]
        s = _dot_nt(kb, q)
        if masked:
            kpos = j * tk + lax.broadcasted_iota(jnp.int32, (tk, 1), 0)
            s = jnp.where(kpos <= qpos, jnp.where(kpos >= lowk, s, NEG_BIG), NEG_BIG)
        smax = jnp.max(s, axis=0, keepdims=True)
        if first:
            m_new = smax
            p = jnp.exp(s - m_new)
            l_ref[...] = jnp.sum(p, axis=0, keepdims=True)
            acc_ref[...] = _dot(vt_ref[j], p.astype(BF16))
        else:
            m_old = m_ref[...]
            m_new = jnp.maximum(m_old, smax)
            a = jnp.exp(m_old - m_new)
            p = jnp.exp(s - m_new)
            l_ref[...] = l_ref[...] * a + jnp.sum(p, axis=0, keepdims=True)
            acc_ref[...] = acc_ref[...] * a + _dot(vt_ref[j], p.astype(BF16))
        m_ref[...] = m_new

    jd = (qi * tq) // tk
    block(0, True, True)

    def body(j, c):
        block(j, False, False)
        return c

    lax.fori_loop(1, jd, body, 0)

    @pl.when(jd > 0)
    def _():
        block(jd, True, False)

    o = acc_ref[...] / l_ref[...]
    for h in range(MLA_HEADS):
        o_ref[:, h * MLA_KV_RANK:(h + 1) * MLA_KV_RANK] = o[:, h * tq:(h + 1) * tq].T.astype(BF16)


def _attn_prompt(qlat, qrope, kcat, vt, n_seq, lp, tq, tk):
    m_rows = tq * MLA_HEADS
    nq = lp // tq
    nk = lp // tk
    dk = MLA_KV_RANK + MLA_ROPE
    qmap = lambda b, i: (b * nq + i, 0)
    return pl.pallas_call(
        functools.partial(_attn_prompt_kernel, tq=tq, tk=tk, lo_key=PAD - N_META),
        grid=(n_seq, nq),
        in_specs=[pl.BlockSpec((tq, MLA_HEADS * MLA_KV_RANK), qmap), pl.BlockSpec((tq, MLA_HEADS * MLA_ROPE), qmap),
                  pl.BlockSpec((lp, dk), lambda b, i: (b, 0)),
                  pl.BlockSpec((nk, MLA_KV_RANK, tk), lambda b, i: (b, 0, 0))],
        out_specs=pl.BlockSpec((tq, MLA_HEADS * MLA_KV_RANK), qmap),
        out_shape=jax.ShapeDtypeStruct(qlat.shape, BF16),
        scratch_shapes=[pltpu.VMEM((1, m_rows), F32), pltpu.VMEM((1, m_rows), F32),
                        pltpu.VMEM((MLA_KV_RANK, m_rows), F32)],
        compiler_params=_params(("parallel", "arbitrary")),
        name="attn_prompt",
    )(qlat, qrope, kcat, vt)


def _attn_sample_kernel(pt_ref, ql_ref, qr_ref, knew_ref, cache_c, cache_r, o_ref,
                        bufc, bufr, semc, semr, m_ref, l_ref, acc_ref, *, n_pages, gp, t_new, page, layer):
    s_id = pl.program_id(0)
    n_seq = pl.num_programs(0)
    ng = n_pages // gp
    gk = gp * page

    def copies(seq, g, slot):
        out = []
        for p in range(gp):
            pg = pt_ref[seq * n_pages + g * gp + p]
            out.append(pltpu.make_async_copy(cache_c.at[layer, pg], bufc.at[slot, pl.ds(p * page, page)],
                                             semc.at[slot]))
            out.append(pltpu.make_async_copy(cache_r.at[layer, pg], bufr.at[slot, pl.ds(p * page, page)],
                                             semr.at[slot]))
        return out

    @pl.when(s_id == 0)
    def _():
        for c in copies(0, 0, 0):
            c.start()

    q = jnp.concatenate([ql_ref[...], qr_ref[...]], axis=1)
    m_ref[...] = jnp.full(m_ref.shape, NEG_BIG, F32)
    l_ref[...] = jnp.zeros(l_ref.shape, F32)
    acc_ref[...] = jnp.zeros(acc_ref.shape, F32)

    def update(s, v):
        m_old = m_ref[...]
        m_new = jnp.maximum(m_old, jnp.max(s, axis=-1, keepdims=True))
        a = jnp.exp(m_old - m_new)
        p = jnp.exp(s - m_new)
        l_ref[...] = l_ref[...] * a + jnp.sum(p, axis=-1, keepdims=True)
        acc_ref[...] = acc_ref[...] * a + _dot(p.astype(BF16), v)
        m_ref[...] = m_new

    def group(g, c):
        it = s_id * ng + g
        slot = it % 2
        nxt_seq = jnp.where(g + 1 < ng, s_id, s_id + 1)
        nxt_g = jnp.where(g + 1 < ng, g + 1, 0)

        @pl.when(nxt_seq < n_seq)
        def _():
            for cp in copies(nxt_seq, nxt_g, 1 - slot):
                cp.start()

        for cp in copies(s_id, g, slot):
            cp.wait()
        kc = bufc[slot].astype(BF16)
        kr = bufr[slot].astype(BF16)
        kb = jnp.concatenate([kc, kr], axis=1)
        update(_dot_nt(q, kb), kc)
        return c

    lax.fori_loop(0, ng, group, 0)

    kn = knew_ref[...]
    s = _dot_nt(q, kn)
    rows = q.shape[0]
    qt = lax.broadcasted_iota(jnp.int32, (rows, t_new), 0) // MLA_HEADS
    kt = lax.broadcasted_iota(jnp.int32, (rows, t_new), 1)
    s = jnp.where(kt <= qt, s, NEG_BIG)
    update(s, kn[:, :MLA_KV_RANK])
    o_ref[...] = (acc_ref[...] / l_ref[...]).astype(BF16)


def _attn_sample(qlat, qrope, kcat, cache_c, cache_r, page_table, t_new, layer):
    n_seq, n_pages = page_table.shape
    page = cache_c.shape[2]
    gp = 16 if n_pages % 16 == 0 else n_pages
    rows = t_new * MLA_HEADS
    dk = MLA_KV_RANK + MLA_ROPE
    ql = qlat.reshape(-1, MLA_KV_RANK)
    qr = qrope.reshape(-1, MLA_ROPE)
    grid_spec = pltpu.PrefetchScalarGridSpec(
        num_scalar_prefetch=1,
        grid=(n_seq,),
        in_specs=[pl.BlockSpec((rows, MLA_KV_RANK), lambda s, pt: (s, 0)),
                  pl.BlockSpec((rows, MLA_ROPE), lambda s, pt: (s, 0)),
                  pl.BlockSpec((t_new, dk), lambda s, pt: (s, 0)),
                  pl.BlockSpec(memory_space=pl.ANY), pl.BlockSpec(memory_space=pl.ANY)],
        out_specs=pl.BlockSpec((rows, MLA_KV_RANK), lambda s, pt: (s, 0)),
        scratch_shapes=[pltpu.VMEM((2, gp * page, MLA_KV_RANK), F32), pltpu.VMEM((2, gp * page, MLA_ROPE), F32),
                        pltpu.SemaphoreType.DMA((2,)), pltpu.SemaphoreType.DMA((2,)),
                        pltpu.VMEM((rows, 1), F32), pltpu.VMEM((rows, 1), F32),
                        pltpu.VMEM((rows, MLA_KV_RANK), F32)],
    )
    out = pl.pallas_call(
        functools.partial(_attn_sample_kernel, n_pages=n_pages, gp=gp, t_new=t_new, page=page, layer=layer),
        grid_spec=grid_spec,
        out_shape=jax.ShapeDtypeStruct(ql.shape, BF16),
        compiler_params=_params(("arbitrary",)),
        name="attn_sample",
    )(page_table.reshape(-1), ql, qr, kcat, cache_c, cache_r)
    return out.reshape(-1, MLA_HEADS * MLA_KV_RANK)


def _level_mats():
    c = CHUNK
    t = np.arange(c)[:, None]
    s = np.arange(c)[None, :]
    mats = [(s <= t)]
    h = c // 2
    while h >= 1:
        start = (t // (2 * h)) * (2 * h) + h
        second = (t % (2 * h)) >= h
        mats.append(np.where(second, (s >= start) & (s <= t), (s > t) & (s < start)))
        h //= 2
    return np.concatenate(mats, axis=0).astype(np.float32)


N_LEVELS = int(np.log2(CHUNK))


def _rec_kernel(*refs, rows, lo_row, hi_row, has_past, precise):
    (gq_ref, gk_ref, gv_ref, gr_ref, mu_ref, mv_ref, mo_ref, misc_ref,
     mats_ref, wa_ref, ba_ref, gnorm_ref, cw_ref, cb_ref, wq_ref, wk_ref, gbias_ref, mnorm_ref, mskip_ref) = refs[:19]
    k = 19
    if has_past:
        st0_ref, ct0_ref, ns0_ref, ms0_ref, cv0_ref = refs[k:k + 5]
        k += 5
    ogla_ref, oml_ref, st_ref, ct_ref, ns_ref, ms_ref, prev_ref = refs[k:k + 7]
    j = pl.program_id(1)
    c = CHUNK
    mm = functools.partial(_mm, precise=precise)

    @pl.when(j == 0)
    def _():
        if has_past:
            st_ref[0] = st0_ref[0]
            ct_ref[0] = ct0_ref[0]
            ns_ref[0] = ns0_ref[0]
            ms_ref[0] = ms0_ref[0]
            prev_ref[...] = cv0_ref[0]
        else:
            st_ref[...] = jnp.zeros(st_ref.shape, F32)
            ct_ref[...] = jnp.zeros(ct_ref.shape, F32)
            ns_ref[...] = jnp.zeros(ns_ref.shape, F32)
            ms_ref[...] = jnp.zeros(ms_ref.shape, F32)
            prev_ref[...] = jnp.zeros(prev_ref.shape, F32)

    def load(ref):
        x = ref[...]
        if rows < c:
            x = jnp.concatenate([x, jnp.zeros((c - rows, x.shape[1]), F32)], axis=0)
        return x

    row_i = lax.broadcasted_iota(jnp.int32, (c, 1), 0)
    grow = j * c + row_i
    valid = (grow >= lo_row) & (grow < hi_row)
    lane128 = lax.broadcasted_iota(jnp.int32, (1, LANE), 1)
    lane256 = lax.broadcasted_iota(jnp.int32, (1, 256), 1)
    t_i = lax.broadcasted_iota(jnp.int32, (c, c), 0)
    s_i = lax.broadcasted_iota(jnp.int32, (c, c), 1)
    causal = s_i <= t_i
    mats = mats_ref[...]
    tri = mats[:c]
    misc = load(misc_ref)

    la = jax.nn.log_sigmoid(mm(misc, wa_ref[...], NN_DIMS) + ba_ref[...]) * (1.0 / GLA_TAU)
    la = jnp.where(valid, la, 0.0)
    gq = load(gq_ref) * (GLA_DK ** -0.5)
    gk = jnp.where(valid, load(gk_ref), 0.0)
    gv = jnp.where(valid, load(gv_ref), 0.0)
    ex = _dot_exact_lhs(mats, la)
    b = ex[:c]
    ghead = [(lane128 >= GLA_DK * h) & (lane128 < GLA_DK * (h + 1)) for h in range(GLA_HEADS)]
    vhead = [(lane256 >= GLA_DV * h) & (lane256 < GLA_DV * (h + 1)) for h in range(GLA_HEADS)]
    tmod = lax.broadcasted_iota(jnp.int32, (GLA_HEADS * c, c), 0) % c
    smod = lax.broadcasted_iota(jnp.int32, (GLA_HEADS * c, c), 1)

    def expand(x):
        return jnp.concatenate([jnp.where(ghead[h], x, 0.0) for h in range(GLA_HEADS)], axis=0)

    a_all = jnp.where(tmod == smod, mm(expand(gq), gk, NT_DIMS), 0.0)
    for lv in range(N_LEVELS):
        e = jnp.exp(ex[(lv + 1) * c:(lv + 2) * c])
        bit = N_LEVELS - 1 - lv
        second = ((row_i >> bit) & 1) == 1
        qt = jnp.where(second, gq * e, 0.0)
        kt = jnp.where(second, 0.0, gk * e)
        a_lv = mm(expand(qt), kt, NT_DIMS)
        a_all = a_all + jnp.where((tmod >> (bit + 1)) == (smod >> (bit + 1)), a_lv, 0.0)
    o = jnp.zeros((c, 256), F32)
    for h in range(GLA_HEADS):
        o = o + jnp.where(vhead[h], mm(a_all[h * c:(h + 1) * c], gv, NN_DIMS), 0.0)
    st = st_ref[0]
    o = o + mm(gq * jnp.exp(b), st, NT_DIMS)
    bl = b[c - 1:c]
    kl = gk * jnp.exp(bl - b)
    bd_gla = ((lax.broadcasted_iota(jnp.int32, (256, LANE), 0) // GLA_DV)
              == (lax.broadcasted_iota(jnp.int32, (256, LANE), 1) // GLA_DK))
    st_ref[0] = st * jnp.exp(bl) + jnp.where(bd_gla, mm(gv, kl, TN_DIMS), 0.0)
    ms2 = jnp.zeros((c, 256), F32)
    for h in range(GLA_HEADS):
        s2 = jnp.sum(jnp.where(vhead[h], o * o, 0.0), axis=-1, keepdims=True) * (1.0 / GLA_DV)
        ms2 = ms2 + jnp.where(vhead[h], lax.rsqrt(s2 + RMS_EPS), 0.0)
    ogla = o * ms2 * gnorm_ref[...] * jax.nn.silu(load(gr_ref))
    ogla_ref[...] = ogla[:rows]

    mu = jnp.where(valid, load(mu_ref), 0.0)
    up = jnp.concatenate([prev_ref[...], mu], axis=0)
    cw = cw_ref[...]
    uc = cb_ref[...]
    for i in range(CONV_W):
        off = 8 - (CONV_W - 1) + i
        uc = uc + up[off:off + c] * cw[i:i + 1]
    uc = jax.nn.silu(uc)
    if rows == c:
        prev_ref[...] = mu[c - 8:]
    mq = mm(uc, wq_ref[...], NN_DIMS)
    mk = mm(uc, wk_ref[...], NN_DIMS) * (ML_DH ** -0.5)
    mv = load(mv_ref)
    g_raw = misc + gbias_ref[...]
    is_i = (lane128 >= MS_MI) & (lane128 < MS_MI + ML_HEADS)
    is_f = (lane128 >= MS_MF) & (lane128 < MS_MF + ML_HEADS)
    g = jnp.where(is_i, jnp.where(valid, g_raw, NEG_BIG),
                  jnp.where(is_f, jnp.where(valid, jax.nn.log_sigmoid(g_raw), 0.0), 0.0))
    fc_all = _dot_exact_lhs(tri, jnp.where(is_f, g, 0.0))
    gt = g.T
    ft = fc_all.T
    ms_old = ms_ref[0]
    ns_old = ns_ref[0][0:1]
    ct = ct_ref[0]
    inter = mm(mq, ct, NT_DIMS)
    qn = mq * ns_old
    hc = jnp.zeros((c, 256), F32)
    ws_full = jnp.zeros((c, 256), F32)
    wl_row = jnp.zeros((1, 256), F32)
    ms_new = jnp.zeros((1, LANE), F32)
    for h in range(ML_HEADS):
        fcol = fc_all[:, MS_MF + h:MS_MF + h + 1]
        frow = ft[MS_MF + h:MS_MF + h + 1]
        igrow = gt[MS_MI + h:MS_MI + h + 1]
        igcol = g[:, MS_MI + h:MS_MI + h + 1]
        m0 = ms_old[0:1, h:h + 1]
        d = jnp.where(causal, fcol - frow + igrow, NEG_BIG)
        a = fcol + m0
        m = jnp.maximum(a, jnp.max(d, axis=-1, keepdims=True))
        wp = jnp.exp(a - m)
        w = jnp.exp(d - m)
        qk = mm(jnp.where(vhead[h], mq, 0.0), mk, NT_DIMS) * w
        num = wp * inter + mm(qk, mv, NN_DIMS)
        den = wp * jnp.sum(jnp.where(vhead[h], qn, 0.0), axis=-1, keepdims=True) + jnp.sum(qk, axis=-1, keepdims=True)
        hh = num / jnp.maximum(jnp.abs(den), jnp.exp(-m))
        hc = hc + jnp.where(vhead[h], hh, 0.0)
        m_last = m[c - 1:c]
        f_last = fcol[c - 1:c]
        wl = jnp.exp(f_last + m0 - m_last)
        ws = jnp.exp(f_last - fcol + igcol - m_last)
        ws_full = ws_full + jnp.where(vhead[h], ws, 0.0)
        wl_row = wl_row + jnp.where(vhead[h], wl, 0.0)
        ms_new = ms_new + jnp.where(lane128 == h, m_last, 0.0)
    kw = mk * ws_full
    bd_ml = ((lax.broadcasted_iota(jnp.int32, (256, 256), 0) // ML_DH)
             == (lax.broadcasted_iota(jnp.int32, (256, 256), 1) // ML_DH))
    ct_ref[0] = ct * wl_row + jnp.where(bd_ml, mm(mv, kw, TN_DIMS), 0.0)
    ns_new = wl_row * ns_old + jnp.sum(kw, axis=0, keepdims=True)
    ns_ref[0] = jnp.broadcast_to(ns_new, (8, 256))
    ms_ref[0] = jnp.broadcast_to(ms_new, (8, LANE))
    hc = hc * jax.nn.sigmoid(load(mo_ref))
    hn = jnp.zeros((c, 256), F32)
    for h in range(ML_HEADS):
        mean = jnp.sum(jnp.where(vhead[h], hc, 0.0), axis=-1, keepdims=True) * (1.0 / ML_DH)
        xc = jnp.where(vhead[h], hc - mean, 0.0)
        var = jnp.sum(xc * xc, axis=-1, keepdims=True) * (1.0 / ML_DH)
        hn = hn + xc * lax.rsqrt(var + LN_EPS)
    oml = hn * mnorm_ref[...] + mskip_ref[...] * uc
    oml_ref[...] = oml[:rows]


def _rec(proj, lw, consts, n_seq, n_blocks, rows, lo_row, hi_row, past, precise):
    n = proj.shape[0]
    has_past = past is not None
    rmap = lambda cidx: (lambda s, j: (s * n_blocks + j, cidx))
    in_specs = [pl.BlockSpec((rows, LANE), rmap(PJ_GQ)), pl.BlockSpec((rows, LANE), rmap(PJ_GK)),
                pl.BlockSpec((rows, 256), rmap(PJ_GV)), pl.BlockSpec((rows, 256), rmap(PJ_GR)),
                pl.BlockSpec((rows, 256), rmap(PJ_MU)), pl.BlockSpec((rows, 256), rmap(PJ_MV)),
                pl.BlockSpec((rows, 256), rmap(PJ_MO)), pl.BlockSpec((rows, LANE), rmap(PJ_MISC))]
    args = [proj] * 8
    weights = [consts["mats"], lw["wa"], lw["ba"], lw["gnorm"], lw["conv_w"], lw["conv_b"], lw["wq_bd"], lw["wk_bd"],
               lw["gbias"], lw["mnorm"], lw["mskip"]]
    in_specs += [_const_spec(w.shape) for w in weights]
    args += weights
    smap = lambda s, j: (s, 0, 0)
    state_shapes = [(n_seq, 256, LANE), (n_seq, 256, 256), (n_seq, 8, 256), (n_seq, 8, LANE)]
    if has_past:
        in_specs += [pl.BlockSpec((1,) + sh[1:], smap) for sh in state_shapes] + [pl.BlockSpec((1, 8, 256), smap)]
        args += list(past)
    out_specs = [pl.BlockSpec((rows, 256), rmap(0)), pl.BlockSpec((rows, 256), rmap(0))]
    out_specs += [pl.BlockSpec((1,) + sh[1:], smap) for sh in state_shapes]
    out_shape = [jax.ShapeDtypeStruct((n, 256), F32), jax.ShapeDtypeStruct((n, 256), F32)]
    out_shape += [jax.ShapeDtypeStruct(sh, F32) for sh in state_shapes]
    return pl.pallas_call(
        functools.partial(_rec_kernel, rows=rows, lo_row=lo_row, hi_row=hi_row, has_past=has_past, precise=precise),
        grid=(n_seq, n_blocks),
        in_specs=in_specs,
        out_specs=out_specs,
        out_shape=out_shape,
        scratch_shapes=[pltpu.VMEM((8, 256), F32)],
        compiler_params=_params(("parallel", "arbitrary")),
        name="rec",
    )(*args)


def _post_kernel(x_ref, olat_ref, ogla_ref, oml_ref, wo_ref, wo2_ref, l1g_ref, l1b_ref, rw_ref, rb_ref,
                 wgu_ref, wd_ref, l2g_ref, l2b_ref, o_ref):
    tm = x_ref.shape[0]
    gm = jnp.concatenate([ogla_ref[...], oml_ref[...]], axis=1)
    gb = gm.astype(BF16)
    if wo2_ref.shape[0] == 3 * gm.shape[1]:
        gb = jnp.concatenate([gb, (gm - gb.astype(F32)).astype(BF16), gb], axis=1)
    y = _dot(olat_ref[...], wo_ref[...]) + _dot(gb, wo2_ref[...])
    x1 = _layer_norm(ALPHA * x_ref[...] + y, l1g_ref[...], l1b_ref[...])
    xb = x1.astype(BF16)

    x_mid = (x1 - xb.astype(F32)).astype(BF16)
    lg = _dot(jnp.concatenate([xb, x_mid, xb], axis=1), rw_ref[...])
    sc = jax.nn.sigmoid(lg).T[:N_EXPERTS]
    sel = sc + rb_ref[...]
    ng, ne = N_GROUPS, E_PER_GROUP
    a_sel = [sel[ng * i:ng * (i + 1)] for i in range(ne)]
    top2 = None
    for i in range(ne):
        for k in range(i + 1, ne):
            ps = a_sel[i] + a_sel[k]
            top2 = ps if top2 is None else jnp.maximum(top2, ps)
    tg = [top2[g:g + 1] for g in range(ng)]
    best = functools.reduce(jnp.maximum, tg)
    gsel, taken = [], jnp.zeros_like(best)
    for g in range(ng):
        hit = jnp.where(tg[g] >= best, 1.0, 0.0) * (1.0 - taken)
        gsel.append(hit)
        taken = taken + hit
    cs = [sum(gsel[g] * jnp.where(gsel[g] > 0.0, sel[ng * i + g:ng * i + g + 1], 0.0) for g in range(ng))
          for i in range(ne)]
    ss = [sum(gsel[g] * jnp.where(gsel[g] > 0.0, sc[ng * i + g:ng * i + g + 1], 0.0) for g in range(ng))
          for i in range(ne)]
    gates = []
    for i in range(ne):
        rank = jnp.zeros_like(best)
        for k in range(ne):
            if k < i:
                rank = rank + jnp.where(cs[k] >= cs[i], 1.0, 0.0)
            elif k > i:
                rank = rank + jnp.where(cs[k] > cs[i], 1.0, 0.0)
        gates.append(jnp.where(rank < 1.5, ss[i], 0.0))
    inv = 1.0 / sum(gates)
    rows = [gates[i] * inv * gsel[g] for i in range(ne) for g in range(ng)]
    wt = jnp.concatenate(rows + [jnp.zeros((LANE - N_EXPERTS, tm), F32)], axis=0)
    wcol = wt.T

    acc = jnp.zeros((tm, x_ref.shape[1]), F32)
    for e in range(N_EXPERTS):
        col = ng * (e % ne) + e // ne
        gu = _dot(xb, wgu_ref[e])
        hh = jax.nn.silu(gu[:, :D_EXPERT]) * gu[:, D_EXPERT:] * wcol[:, col:col + 1]
        acc = acc + _dot(hh.astype(BF16), wd_ref[e])
    o_ref[...] = _layer_norm(ALPHA * x1 + acc, l2g_ref[...], l2b_ref[...])


def _post(x, olat, ogla, oml, lw, rw, rb, tm, precise):
    n, d = x.shape
    row = lambda i: (i, 0)
    single = dict(pipeline_mode=pl.Buffered(1))
    consts = [lw["wo"], lw["wo2_3"] if precise else lw["wo2"], lw["ln1_g"], lw["ln1_b"], rw, rb, lw["wgu"], lw["wd"], lw["ln2_g"], lw["ln2_b"]]
    in_specs = [pl.BlockSpec((tm, d), row), pl.BlockSpec((tm, 1024), row), pl.BlockSpec((tm, 256), row),
                pl.BlockSpec((tm, 256), row)]
    for w in consts:
        nd = w.ndim
        in_specs.append(pl.BlockSpec(w.shape, (lambda i, _nd=nd: (0,) * _nd), **single))
    return pl.pallas_call(
        _post_kernel,
        grid=(n // tm,),
        in_specs=in_specs,
        out_specs=pl.BlockSpec((tm, d), row),
        out_shape=jax.ShapeDtypeStruct((n, d), F32),
        compiler_params=_params(("parallel",)),
        name="post",
    )(x, olat, ogla, oml, *consts)


def _rope_tables(pos):
    half = MLA_ROPE // 2
    inv = ROPE_THETA ** (-jnp.arange(half, dtype=F32) / half)
    ang = pos.astype(F32)[:, None] * inv
    cos, sin = jnp.cos(ang), jnp.sin(ang)
    c2 = jnp.concatenate([cos, cos], axis=1)
    s2 = jnp.concatenate([-sin, sin], axis=1)
    cq = jnp.tile(c2, (1, MLA_HEADS))
    sq = jnp.tile(s2, (1, MLA_HEADS))
    kt = jnp.concatenate([c2, s2, jnp.zeros((pos.shape[0], LANE - 2 * MLA_ROPE), F32)], axis=1)
    return cq, sq, kt


def _prep_layer(l, p):
    d = p["w_in"].shape[1]
    offs = np.concatenate([[0], np.cumsum(IN_SIZES)])
    wi = p["w_in"][l]
    cq, ckv, kr, gq, gk, gv, ga, gr, mu, mv, mi, mf, mo = [wi[:, offs[i]:offs[i + 1]] for i in range(13)]
    half = MLA_ROPE // 2
    kr_sw = jnp.concatenate([kr[:, half:], kr[:, :half]], axis=1)
    misc = jnp.concatenate([kr, kr_sw, ga, mi, mf, jnp.zeros((d, LANE - MS_MF - ML_HEADS), F32)], axis=1)
    w_in32 = jnp.concatenate([cq, gv, gr, mu, mv, mo, ckv, gq, gk, misc], axis=1)
    w_in = w_in32.astype(BF16)
    w_in3 = _split_weight(w_in32) if l == 0 else None

    dq = MLA_NOPE + MLA_ROPE
    wuq = p["mla_w_uq"][l].reshape(MLA_Q_RANK, MLA_HEADS, dq)
    wukv = p["mla_w_ukv"][l].reshape(MLA_KV_RANK, MLA_HEADS, MLA_NOPE + MLA_V)
    wuk_t = jnp.transpose(wukv[..., :MLA_NOPE], (1, 2, 0))
    wql = _fold(jnp.transpose(wuq[..., :MLA_NOPE], (1, 0, 2)), wuk_t)
    wql = jnp.transpose(wql, (1, 0, 2)).reshape(MLA_Q_RANK, MLA_HEADS * MLA_KV_RANK)
    wqr = wuq[..., MLA_NOPE:]
    wqr_sw = jnp.concatenate([wqr[..., half:], wqr[..., :half]], axis=-1)
    wq = jnp.concatenate([wql, wqr.reshape(MLA_Q_RANK, -1), wqr_sw.reshape(MLA_Q_RANK, -1)], axis=1).astype(BF16)

    wout = p["w_out"][l]
    nv = MLA_HEADS * MLA_V
    wuv = jnp.transpose(wukv[..., MLA_NOPE:], (1, 0, 2))
    wo_mla = _fold(wuv, wout[:nv].reshape(MLA_HEADS, MLA_V, d)).reshape(MLA_HEADS * MLA_KV_RANK, d)
    wo = wo_mla.astype(BF16)
    wo2 = wout[nv:].astype(BF16)
    wo2_3 = _split_weight(wout[nv:]) if l == 0 else None

    wa = jnp.zeros((LANE, LANE), F32).at[MS_GA:MS_GA + GLA_GATE_RANK].set(p["gla_w_a"][l])
    gbias = (jnp.zeros((1, LANE), F32).at[0, MS_MI:MS_MI + ML_HEADS].set(p["ml_b_i"][l])
             .at[0, MS_MF:MS_MF + ML_HEADS].set(p["ml_b_f"][l]))

    def block_diag(w):
        out = jnp.zeros((ML_WIDTH, ML_WIDTH), F32)
        for h in range(ML_HEADS):
            out = out.at[h * ML_DH:(h + 1) * ML_DH, h * ML_DH:(h + 1) * ML_DH].set(w[h])
        return out

    conv_w = jnp.concatenate([p["ml_conv_w"][l], jnp.zeros((8 - CONV_W, ML_WIDTH), F32)], axis=0)
    wgu = jnp.concatenate([p["moe_w_gate"][l], p["moe_w_up"][l]], axis=2).astype(BF16)
    return dict(
        w_in=w_in, w_in3=w_in3, wq=wq, wo=wo, wo2=wo2, wo2_3=wo2_3, wa=wa, gbias=gbias,
        qn=p["mla_q_norm"][l][None], kvn=p["mla_kv_norm"][l][None],
        ba=p["gla_b_a"][l][None], gnorm=jnp.tile(p["gla_norm"][l], GLA_HEADS)[None],
        conv_w=conv_w, conv_b=p["ml_conv_b"][l][None], wq_bd=block_diag(p["ml_w_q"][l]),
        wk_bd=block_diag(p["ml_w_k"][l]), mnorm=p["ml_norm"][l][None], mskip=p["ml_skip"][l][None],
        ln1_g=p["ln1_g"][l][None], ln1_b=p["ln1_b"][l][None], ln2_g=p["ln2_g"][l][None], ln2_b=p["ln2_b"][l][None],
        wgu=wgu, wd=p["moe_w_down"][l].astype(BF16),
    )


def _gla_state_out(st):
    n = st.shape[0]
    s5 = st.reshape(n, GLA_HEADS, GLA_DV, GLA_HEADS, GLA_DK)
    return jnp.stack([jnp.transpose(s5[:, h, :, h, :], (0, 2, 1)) for h in range(GLA_HEADS)], axis=1)


def _gla_state_in(s):
    n = s.shape[0]
    out = jnp.zeros((n, GLA_HEADS, GLA_DV, GLA_HEADS, GLA_DK), F32)
    for h in range(GLA_HEADS):
        out = out.at[:, h, :, h, :].set(jnp.transpose(s[:, h], (0, 2, 1)))
    return out.reshape(n, GLA_HEADS * GLA_DV, GLA_HEADS * GLA_DK)


def _ml_state_out(ct):
    n = ct.shape[0]
    c5 = ct.reshape(n, ML_HEADS, ML_DH, ML_HEADS, ML_DH)
    return jnp.stack([jnp.transpose(c5[:, h, :, h, :], (0, 2, 1)) for h in range(ML_HEADS)], axis=1)


def _ml_state_in(cs):
    n = cs.shape[0]
    out = jnp.zeros((n, ML_HEADS, ML_DH, ML_HEADS, ML_DH), F32)
    for h in range(ML_HEADS):
        out = out.at[:, h, :, h, :].set(jnp.transpose(cs[:, h], (0, 2, 1)))
    return out.reshape(n, ML_WIDTH, ML_WIDTH)


def _pick_tile(n, cands):
    for t in cands:
        if n % t == 0:
            return t
    raise ValueError(f"no row tile for {n} rows")


def kernel(x_prompt, x_sample, cache_kv_latent, cache_k_rope, state_gla, state_mlstm_C, state_mlstm_n, state_mlstm_m, state_mlstm_conv, page_table, meta_tokens, ln_in_g, ln_in_b, w_in, mla_q_norm, mla_w_uq, mla_kv_norm, mla_w_ukv, gla_w_a, gla_b_a, gla_norm, ml_conv_w, ml_conv_b, ml_w_q, ml_w_k, ml_b_i, ml_b_f, ml_norm, ml_skip, w_out, ln1_g, ln1_b, ln2_g, ln2_b, router_w, router_bias, moe_w_gate, moe_w_up, moe_w_down):
    p = dict(w_in=w_in, mla_q_norm=mla_q_norm, mla_w_uq=mla_w_uq, mla_kv_norm=mla_kv_norm, mla_w_ukv=mla_w_ukv,
             gla_w_a=gla_w_a, gla_b_a=gla_b_a, gla_norm=gla_norm, ml_conv_w=ml_conv_w, ml_conv_b=ml_conv_b,
             ml_w_q=ml_w_q, ml_w_k=ml_w_k, ml_b_i=ml_b_i, ml_b_f=ml_b_f, ml_norm=ml_norm, ml_skip=ml_skip,
             w_out=w_out, ln1_g=ln1_g, ln1_b=ln1_b, ln2_g=ln2_g, ln2_b=ln2_b, moe_w_gate=moe_w_gate,
             moe_w_up=moe_w_up, moe_w_down=moe_w_down)
    depth = w_in.shape[0]
    nb, seq, d = x_prompt.shape
    ns, t_new, _ = x_sample.shape
    n_pages = page_table.shape[1]
    page = cache_kv_latent.shape[2]
    past_len = n_pages * page
    lp = PAD + seq
    assert seq % CHUNK == 0 and t_new == 8 and t_new >= CONV_W - 1

    layers = [_prep_layer(l, p) for l in range(depth)]
    consts = dict(mats=jnp.asarray(_level_mats(), BF16))
    perm = np.array([E_PER_GROUP * g + i for i in range(E_PER_GROUP) for g in range(N_GROUPS)])
    rw_hi, rw_mid = _split2_bits(router_w.astype(F32)[:, perm])
    rw = jnp.pad(jnp.concatenate([rw_hi, rw_hi, rw_mid], axis=0), ((0, 0), (0, LANE - N_EXPERTS)))
    rb = router_bias.astype(F32)[perm][:, None]
    g_in, b_in = ln_in_g[None], ln_in_b[None]

    meta = jnp.broadcast_to(meta_tokens[None].astype(F32), (nb, N_META, d))
    xp = jnp.concatenate([jnp.zeros((nb, PAD - N_META, d), F32), meta, x_prompt], axis=1).reshape(nb * lp, d)
    tabs_p = _rope_tables(jnp.arange(lp) - (PAD - N_META))
    tm_p = _pick_tile(nb * lp, (512, 384, 256, 128))
    tk_p = _pick_tile(lp, (384, 128))
    p_out = dict(kv=[], kr=[], gla=[], C=[], n=[], m=[], conv=[])
    for l in range(depth):
        lw = layers[l]
        precise = l == 0
        proj, xp = _in_proj(xp, g_in, b_in, lw["w_in3"] if precise else lw["w_in"], l == 0, tm_p)
        qlat, qrope, ckvn, krope, kcat, vt = _mla_prep(proj, tabs_p, lw["qn"], lw["kvn"], lw["wq"], tk_p, lp // tk_p)
        olat = _attn_prompt(qlat, qrope, kcat, vt, nb, lp, CHUNK, tk_p)
        ogla, oml, st, ct, nst, mst = _rec(proj, lw, consts, nb, lp // CHUNK, CHUNK, PAD - N_META, lp, None, precise)
        xp = _post(xp, olat, ogla, oml, lw, rw, rb, tm_p, precise)
        lo = PAD - N_META
        p_out["kv"].append(ckvn.reshape(nb, lp, -1)[:, lo:])
        p_out["kr"].append(krope.reshape(nb, lp, -1)[:, lo:])
        p_out["gla"].append(_gla_state_out(st))
        p_out["C"].append(_ml_state_out(ct))
        p_out["n"].append(nst[:, 0].reshape(nb, ML_HEADS, ML_DH))
        p_out["m"].append(mst[:, 0, :ML_HEADS])
        mu_cols = proj[:, PJ_MU * 256:(PJ_MU + 1) * 256].reshape(nb, lp, ML_WIDTH)
        p_out["conv"].append(mu_cols[:, lp - (CONV_W - 1):])
    y_prompt = xp.reshape(nb, lp, d)[:, PAD:]

    xs = x_sample.reshape(ns * t_new, d)
    tabs_s = _rope_tables(past_len + jnp.arange(t_new))
    tm_s = _pick_tile(ns * t_new, (512, 256, 128, 64, 32, 16))
    tabs_s = tuple(jnp.tile(t, (tm_s // t_new, 1)) for t in tabs_s)
    s_out = dict(kv=[], kr=[], gla=[], C=[], n=[], m=[], conv=[])
    for l in range(depth):
        lw = layers[l]
        proj, xs = _in_proj(xs, g_in, b_in, lw["w_in"], l == 0, tm_s)
        qlat, qrope, ckvn, krope, kcat, _ = _mla_prep(proj, tabs_s, lw["qn"], lw["kvn"], lw["wq"], tm_s, 1)
        olat = _attn_sample(qlat, qrope, kcat, cache_kv_latent, cache_k_rope, page_table, t_new, l)
        ns0 = jnp.broadcast_to(state_mlstm_n[l].reshape(ns, 1, ML_WIDTH), (ns, 8, ML_WIDTH))
        ms0 = jnp.broadcast_to(jnp.pad(state_mlstm_m[l], ((0, 0), (0, LANE - ML_HEADS)))[:, None], (ns, 8, LANE))
        cv0 = jnp.pad(state_mlstm_conv[l], ((0, 0), (8 - (CONV_W - 1), 0), (0, 0)))
        past = (_gla_state_in(state_gla[l]), _ml_state_in(state_mlstm_C[l]), ns0, ms0, cv0)
        ogla, oml, st, ct, nst, mst = _rec(proj, lw, consts, ns, 1, t_new, 0, t_new, past, False)
        xs = _post(xs, olat, ogla, oml, lw, rw, rb, tm_s, False)
        s_out["kv"].append(ckvn.reshape(ns, t_new, -1))
        s_out["kr"].append(krope.reshape(ns, t_new, -1))
        s_out["gla"].append(_gla_state_out(st))
        s_out["C"].append(_ml_state_out(ct))
        s_out["n"].append(nst[:, 0].reshape(ns, ML_HEADS, ML_DH))
        s_out["m"].append(mst[:, 0, :ML_HEADS])
        mu_cols = proj[:, PJ_MU * 256:(PJ_MU + 1) * 256].reshape(ns, t_new, ML_WIDTH)
        s_out["conv"].append(mu_cols[:, t_new - (CONV_W - 1):])
    y_sample = xs.reshape(ns, t_new, d)

    st_p = [jnp.stack(p_out[k]) for k in ("kv", "kr", "gla", "C", "n", "m", "conv")]
    st_s = [jnp.stack(s_out[k]) for k in ("kv", "kr", "gla", "C", "n", "m", "conv")]
    return (y_prompt, y_sample, *st_p, *st_s)
```

```python
import functools

import numpy as np
import jax
import jax.numpy as jnp
from jax import lax
from jax.experimental import pallas as pl
from jax.experimental.pallas import tpu as pltpu

F32 = jnp.float32
BF16 = jnp.bfloat16

N_META = 16
MLA_HEADS = 8
MLA_Q_RANK = 256
MLA_KV_RANK = 128
MLA_NOPE = 64
MLA_ROPE = 32
MLA_V = 64
MLA_SCALE = (MLA_NOPE + MLA_ROPE) ** -0.5
ROPE_THETA = 10000.0
GLA_HEADS = 4
GLA_DK = 32
GLA_DV = 64
GLA_GATE_RANK = 16
GLA_TAU = 16.0
ML_HEADS = 4
ML_DH = 64
ML_WIDTH = ML_HEADS * ML_DH
CONV_W = 4
N_EXPERTS = 16
N_GROUPS = 4
E_PER_GROUP = N_EXPERTS // N_GROUPS
D_EXPERT = 256
DEPTH = 2
ALPHA = (2 * DEPTH) ** 0.25
LN_EPS = 1e-5
RMS_EPS = 1e-6
IN_SIZES = (MLA_Q_RANK, MLA_KV_RANK, MLA_ROPE,
            GLA_HEADS * GLA_DK, GLA_HEADS * GLA_DK, GLA_HEADS * GLA_DV, GLA_GATE_RANK, GLA_HEADS * GLA_DV,
            ML_WIDTH, ML_WIDTH, ML_HEADS, ML_HEADS, ML_WIDTH)

LANE = 128
CHUNK = 128
PAD = 128
D_PROJ = 2048
PJ_CQ, PJ_GV, PJ_GR, PJ_MU, PJ_MV, PJ_MO = 0, 1, 2, 3, 4, 5
PJ_CKV, PJ_GQ, PJ_GK, PJ_MISC = 12, 13, 14, 15
MS_KR, MS_KRSW, MS_GA, MS_MI, MS_MF = 0, 32, 64, 80, 84
NEG_BIG = -1e30
VMEM_LIMIT = 56 * 1024 * 1024

NT_DIMS = (((1,), (1,)), ((), ()))
TN_DIMS = (((0,), (0,)), ((), ()))


def _dot(a, b):
    return jnp.dot(a, b, preferred_element_type=F32)


def _dot_nt(a, b):
    return lax.dot_general(a, b, NT_DIMS, preferred_element_type=F32)


def _dot_tn(a, b):
    return lax.dot_general(a, b, TN_DIMS, preferred_element_type=F32)


NN_DIMS = (((1,), (0,)), ((), ()))


def _mm(a, b, dims, precise):
    ah, bh = a.astype(BF16), b.astype(BF16)
    out = lax.dot_general(ah, bh, dims, preferred_element_type=F32)
    if precise:
        am = (a - ah.astype(F32)).astype(BF16)
        bm = (b - bh.astype(F32)).astype(BF16)
        out = (out + lax.dot_general(am, bh, dims, preferred_element_type=F32)
               + lax.dot_general(ah, bm, dims, preferred_element_type=F32))
    return out


def _split3(x):
    hi = x.astype(BF16)
    r = x - hi.astype(F32)
    mid = r.astype(BF16)
    lo = (r - mid.astype(F32)).astype(BF16)
    return hi, mid, lo


def _dot_exact_lhs(m01, x):
    hi, mid, lo = _split3(x)
    return _dot(m01, hi) + _dot(m01, mid) + _dot(m01, lo)


def _layer_norm(x, g, b):
    mu = jnp.mean(x, axis=-1, keepdims=True)
    xc = x - mu
    var = jnp.mean(xc * xc, axis=-1, keepdims=True)
    return xc * lax.rsqrt(var + LN_EPS) * g + b


def _params(sem):
    return pltpu.CompilerParams(dimension_semantics=sem, vmem_limit_bytes=VMEM_LIMIT)


def _const_spec(shape):
    nd = len(shape)
    return pl.BlockSpec(shape, lambda *_: (0,) * nd)


def _fold_kernel(a_ref, b_ref, o_ref):
    o_ref[0] = jnp.dot(a_ref[0], b_ref[0], preferred_element_type=F32, precision=lax.Precision.HIGHEST)


def _fold(a, b):
    h, m, k = a.shape
    n = b.shape[2]
    return pl.pallas_call(
        _fold_kernel,
        grid=(h,),
        in_specs=[pl.BlockSpec((1, m, k), lambda i: (i, 0, 0)), pl.BlockSpec((1, k, n), lambda i: (i, 0, 0))],
        out_specs=pl.BlockSpec((1, m, n), lambda i: (i, 0, 0)),
        out_shape=jax.ShapeDtypeStruct((h, m, n), F32),
        compiler_params=_params(("arbitrary",)),
        name="fold",
    )(a, b)


def _in_proj_kernel(x_ref, g_ref, b_ref, w_ref, proj_ref, *xn_refs, ln_in):
    x = x_ref[...]
    if ln_in:
        x = _layer_norm(x, g_ref[...], b_ref[...])
        xn_refs[0][...] = x
    xb = x.astype(BF16)
    if w_ref.shape[0] == 3 * x.shape[1]:
        x_mid = (x - xb.astype(F32)).astype(BF16)
        xb = jnp.concatenate([xb, x_mid, xb], axis=1)
    proj_ref[...] = _dot(xb, w_ref[...])


def _split2_bits(w):
    bits = lax.bitcast_convert_type(w.astype(F32), jnp.uint32) & jnp.uint32(0xFFFF0000)
    hi = lax.bitcast_convert_type(bits, F32)
    return hi.astype(BF16), (w - hi).astype(BF16)


def _split_weight(w):
    hi, mid = _split2_bits(w)
    return jnp.concatenate([hi, hi, mid], axis=0)


def _in_proj(x, g, b, w, ln_in, tm):
    n, d = x.shape
    out_shape = [jax.ShapeDtypeStruct((n, D_PROJ), F32)]
    out_specs = [pl.BlockSpec((tm, D_PROJ), lambda i: (i, 0))]
    if ln_in:
        out_shape.append(jax.ShapeDtypeStruct((n, d), F32))
        out_specs.append(pl.BlockSpec((tm, d), lambda i: (i, 0)))
    res = pl.pallas_call(
        functools.partial(_in_proj_kernel, ln_in=ln_in),
        grid=(n // tm,),
        in_specs=[pl.BlockSpec((tm, d), lambda i: (i, 0)), _const_spec((1, d)), _const_spec((1, d)),
                  pl.BlockSpec(w.shape, lambda i: (0, 0), pipeline_mode=pl.Buffered(1))],
        out_specs=out_specs,
        out_shape=out_shape,
        compiler_params=_params(("parallel",)),
        name="in_proj",
    )(x, g, b, w)
    return (res[0], res[1]) if ln_in else (res[0], x)


def _mla_prep_kernel(cq_ref, ckv_ref, misc_ref, cq_tab, sq_tab, k_tab, qn_ref, kvn_ref, wq_ref,
                     qlat_ref, qrope_ref, ckvn_ref, krope_ref, kcat_ref, vt_ref):
    cq = cq_ref[...]
    cqn = cq * lax.rsqrt(jnp.mean(cq * cq, axis=-1, keepdims=True) + RMS_EPS) * qn_ref[...]
    q = _dot(cqn.astype(BF16), wq_ref[...])
    nl = MLA_HEADS * MLA_KV_RANK
    nr = MLA_HEADS * MLA_ROPE
    qlat_ref[...] = (q[:, :nl] * MLA_SCALE).astype(BF16)
    rot = q[:, nl:nl + nr] * cq_tab[...] + q[:, nl + nr:] * sq_tab[...]
    qrope_ref[...] = (rot * MLA_SCALE).astype(BF16)
    ckv = ckv_ref[...]
    ckvn = ckv * lax.rsqrt(jnp.mean(ckv * ckv, axis=-1, keepdims=True) + RMS_EPS) * kvn_ref[...]
    ckvn_ref[...] = ckvn
    p = misc_ref[...] * k_tab[...]
    kr = p[:, MS_KR:MS_KR + MLA_ROPE] + p[:, MS_KRSW:MS_KRSW + MLA_ROPE]
    krope_ref[...] = kr
    kcat_ref[...] = jnp.concatenate([ckvn, kr], axis=1).astype(BF16)
    vt_ref[0] = ckvn.T.astype(BF16)


def _mla_prep(proj, tabs, qn, kvn, wq, tp, tab_blocks):
    n = proj.shape[0]
    cq_tab, sq_tab, k_tab = tabs
    nq = wq.shape[1]
    dk = MLA_KV_RANK + MLA_ROPE
    row = lambda c: (lambda i: (i, c))
    tab = lambda i: (i % tab_blocks, 0)
    return pl.pallas_call(
        _mla_prep_kernel,
        grid=(n // tp,),
        in_specs=[pl.BlockSpec((tp, 256), row(PJ_CQ)), pl.BlockSpec((tp, LANE), row(PJ_CKV)),
                  pl.BlockSpec((tp, LANE), row(PJ_MISC)),
                  pl.BlockSpec((tp, 256), tab), pl.BlockSpec((tp, 256), tab), pl.BlockSpec((tp, LANE), tab),
                  _const_spec((1, MLA_Q_RANK)), _const_spec((1, MLA_KV_RANK)), _const_spec((MLA_Q_RANK, nq))],
        out_specs=[pl.BlockSpec((tp, 1024), row(0)), pl.BlockSpec((tp, 256), row(0)),
                   pl.BlockSpec((tp, MLA_KV_RANK), row(0)), pl.BlockSpec((tp, MLA_ROPE), row(0)),
                   pl.BlockSpec((tp, dk), row(0)), pl.BlockSpec((1, MLA_KV_RANK, tp), lambda i: (i, 0, 0))],
        out_shape=[jax.ShapeDtypeStruct((n, 1024), BF16), jax.ShapeDtypeStruct((n, 256), BF16),
                   jax.ShapeDtypeStruct((n, MLA_KV_RANK), F32), jax.ShapeDtypeStruct((n, MLA_ROPE), F32),
                   jax.ShapeDtypeStruct((n, dk), BF16), jax.ShapeDtypeStruct((n // tp, MLA_KV_RANK, tp), BF16)],
        compiler_params=_params(("parallel",)),
        name="mla_prep",
    )(proj, proj, proj, cq_tab, sq_tab, k_tab, qn, kvn, wq)


def _attn_prompt_kernel(ql_ref, qr_ref, k_ref, vt_ref, o_ref, m_ref, l_ref, acc_ref, *, tq, tk, lo_key):
    qi = pl.program_id(1)
    m_rows = tq * MLA_HEADS
    ql = ql_ref[...]
    qr = qr_ref[...]
    q = jnp.concatenate(
        [jnp.concatenate([ql[:, h * MLA_KV_RANK:(h + 1) * MLA_KV_RANK], qr[:, h * MLA_ROPE:(h + 1) * MLA_ROPE]], axis=1)
         for h in range(MLA_HEADS)], axis=0)
    qpos = qi * tq + lax.broadcasted_iota(jnp.int32, (1, m_rows), 1) % tq
    lowk = jnp.where(qpos < lo_key, 0, lo_key)

    def scores(j):
        start = j * tk if isinstance(j, int) else pl.multiple_of(j * tk, tk)
        return _dot_nt(k_ref[pl.ds(start, tk), :], q)

    def consume(s, j, masked, first):
        if masked:
            kpos = j * tk + lax.broadcasted_iota(jnp.int32, (tk, 1), 0)
            s = jnp.where(kpos <= qpos, jnp.where(kpos >= lowk, s, NEG_BIG), NEG_BIG)
        smax = jnp.max(s, axis=0, keepdims=True)
        if first:
            m_new = smax
            p = jnp.exp(s - m_new)
            l_ref[...] = jnp.sum(p, axis=0, keepdims=True)
            acc_ref[...] = _dot(vt_ref[j], p.astype(BF16))
        else:
            m_old = m_ref[...]
            m_new = jnp.maximum(m_old, smax)
            a = jnp.exp(m_old - m_new)
            p = jnp.exp(s - m_new)
            l_ref[...] = l_ref[...] * a + jnp.sum(p, axis=0, keepdims=True)
            acc_ref[...] = acc_ref[...] * a + _dot(vt_ref[j], p.astype(BF16))
        m_ref[...] = m_new

    jd = (qi * tq) // tk
    consume(scores(0), 0, True, True)

    def body(j, c):
        consume(scores(j), j, False, False)
        return c

    lax.fori_loop(1, jd, body, 0)

    @pl.when(jd > 0)
    def _():
        consume(scores(jd), jd, True, False)

    o = acc_ref[...] / l_ref[...]
    for h in range(MLA_HEADS):
        o_ref[:, h * MLA_KV_RANK:(h + 1) * MLA_KV_RANK] = o[:, h * tq:(h + 1) * tq].T.astype(BF16)


def _attn_prompt(qlat, qrope, kcat, vt, n_seq, lp, tq, tk):
    m_rows = tq * MLA_HEADS
    nq = lp // tq
    nk = lp // tk
    dk = MLA_KV_RANK + MLA_ROPE
    qmap = lambda b, i: (b * nq + i, 0)
    return pl.pallas_call(
        functools.partial(_attn_prompt_kernel, tq=tq, tk=tk, lo_key=PAD - N_META),
        grid=(n_seq, nq),
        in_specs=[pl.BlockSpec((tq, MLA_HEADS * MLA_KV_RANK), qmap), pl.BlockSpec((tq, MLA_HEADS * MLA_ROPE), qmap),
                  pl.BlockSpec((lp, dk), lambda b, i: (b, 0)),
                  pl.BlockSpec((nk, MLA_KV_RANK, tk), lambda b, i: (b, 0, 0))],
        out_specs=pl.BlockSpec((tq, MLA_HEADS * MLA_KV_RANK), qmap),
        out_shape=jax.ShapeDtypeStruct(qlat.shape, BF16),
        scratch_shapes=[pltpu.VMEM((1, m_rows), F32), pltpu.VMEM((1, m_rows), F32),
                        pltpu.VMEM((MLA_KV_RANK, m_rows), F32)],
        compiler_params=_params(("parallel", "arbitrary")),
        name="attn_prompt",
    )(qlat, qrope, kcat, vt)


def _attn_sample_kernel(pt_ref, ql_ref, qr_ref, knew_ref, cache_c, cache_r, o_ref,
                        bufc, bufr, semc, semr, m_ref, l_ref, acc_ref, *, n_pages, gp, t_new, page, layer):
    s_id = pl.program_id(0)
    n_seq = pl.num_programs(0)
    ng = n_pages // gp
    gk = gp * page

    def copies(seq, g, slot):
        out = []
        for p in range(gp):
            pg = pt_ref[seq * n_pages + g * gp + p]
            out.append(pltpu.make_async_copy(cache_c.at[layer, pg], bufc.at[slot, pl.ds(p * page, page)],
                                             semc.at[slot]))
            out.append(pltpu.make_async_copy(cache_r.at[layer, pg], bufr.at[slot, p], semr.at[slot]))
        return out

    @pl.when(s_id == 0)
    def _():
        for c in copies(0, 0, 0):
            c.start()

    ql = ql_ref[...]
    qr = qr_ref[...]
    q = jnp.concatenate([ql, qr], axis=1)
    m_ref[...] = jnp.full(m_ref.shape, NEG_BIG, F32)
    l_ref[...] = jnp.zeros(l_ref.shape, F32)
    acc_ref[...] = jnp.zeros(acc_ref.shape, F32)

    def update(s, v):
        m_old = m_ref[...]
        m_new = jnp.maximum(m_old, jnp.max(s, axis=-1, keepdims=True))
        a = jnp.exp(m_old - m_new)
        p = jnp.exp(s - m_new)
        l_ref[...] = l_ref[...] * a + jnp.sum(p, axis=-1, keepdims=True)
        acc_ref[...] = acc_ref[...] * a + _dot(p.astype(BF16), v)
        m_ref[...] = m_new

    def group(g, c):
        it = s_id * ng + g
        slot = it % 2
        nxt_seq = jnp.where(g + 1 < ng, s_id, s_id + 1)
        nxt_g = jnp.where(g + 1 < ng, g + 1, 0)

        @pl.when(nxt_seq < n_seq)
        def _():
            for cp in copies(nxt_seq, nxt_g, 1 - slot):
                cp.start()

        for cp in copies(s_id, g, slot):
            cp.wait()
        kc = bufc[slot].astype(BF16)
        krt = jnp.concatenate([bufr[slot, p] for p in range(gp)], axis=1).astype(BF16)
        update(_dot_nt(ql, kc) + _dot(qr, krt), kc)
        return c

    lax.fori_loop(0, ng, group, 0)

    kn = knew_ref[...]
    s = _dot_nt(q, kn)
    rows = q.shape[0]
    qt = lax.broadcasted_iota(jnp.int32, (rows, t_new), 0) // MLA_HEADS
    kt = lax.broadcasted_iota(jnp.int32, (rows, t_new), 1)
    s = jnp.where(kt <= qt, s, NEG_BIG)
    update(s, kn[:, :MLA_KV_RANK])
    o_ref[...] = (acc_ref[...] / l_ref[...]).astype(BF16)


def _attn_sample(qlat, qrope, kcat, cache_c, cache_r, page_table, t_new, layer):
    n_seq, n_pages = page_table.shape
    page = cache_c.shape[2]
    gp = _pick_tile(n_pages, (64, 32, 16, 8, 4, 2, 1))
    rows = t_new * MLA_HEADS
    dk = MLA_KV_RANK + MLA_ROPE
    ql = qlat.reshape(-1, MLA_KV_RANK)
    qr = qrope.reshape(-1, MLA_ROPE)
    grid_spec = pltpu.PrefetchScalarGridSpec(
        num_scalar_prefetch=1,
        grid=(n_seq,),
        in_specs=[pl.BlockSpec((rows, MLA_KV_RANK), lambda s, pt: (s, 0)),
                  pl.BlockSpec((rows, MLA_ROPE), lambda s, pt: (s, 0)),
                  pl.BlockSpec((t_new, dk), lambda s, pt: (s, 0)),
                  pl.BlockSpec(memory_space=pl.ANY), pl.BlockSpec(memory_space=pl.ANY)],
        out_specs=pl.BlockSpec((rows, MLA_KV_RANK), lambda s, pt: (s, 0)),
        scratch_shapes=[pltpu.VMEM((2, gp * page, MLA_KV_RANK), F32), pltpu.VMEM((2, gp, MLA_ROPE, page), F32),
                        pltpu.SemaphoreType.DMA((2,)), pltpu.SemaphoreType.DMA((2,)),
                        pltpu.VMEM((rows, 1), F32), pltpu.VMEM((rows, 1), F32),
                        pltpu.VMEM((rows, MLA_KV_RANK), F32)],
    )
    out = pl.pallas_call(
        functools.partial(_attn_sample_kernel, n_pages=n_pages, gp=gp, t_new=t_new, page=page, layer=layer),
        grid_spec=grid_spec,
        out_shape=jax.ShapeDtypeStruct(ql.shape, BF16),
        compiler_params=_params(("arbitrary",)),
        name="attn_sample",
    )(page_table.reshape(-1), ql, qr, kcat, cache_c, cache_r)
    return out.reshape(-1, MLA_HEADS * MLA_KV_RANK)


def _level_mats():
    c = CHUNK
    t = np.arange(c)[:, None]
    s = np.arange(c)[None, :]
    mats = [(s <= t)]
    h = c // 2
    while h >= 1:
        start = (t // (2 * h)) * (2 * h) + h
        second = (t % (2 * h)) >= h
        mats.append(np.where(second, (s >= start) & (s <= t), (s > t) & (s < start)))
        h //= 2
    return np.concatenate(mats, axis=0).astype(np.float32)


N_LEVELS = int(np.log2(CHUNK))


def _rec_kernel(*refs, rows, lo_row, hi_row, has_past, precise):
    (gq_ref, gk_ref, gv_ref, gr_ref, mu_ref, mv_ref, mo_ref, misc_ref,
     mats_ref, wa_ref, ba_ref, gnorm_ref, cw_ref, cb_ref, wq_ref, wk_ref, gbias_ref, mnorm_ref, mskip_ref) = refs[:19]
    k = 19
    if has_past:
        st0_ref, ct0_ref, ns0_ref, ms0_ref, cv0_ref = refs[k:k + 5]
        k += 5
    ogla_ref, oml_ref, st_ref, ct_ref, ns_ref, ms_ref, prev_ref = refs[k:k + 7]
    j = pl.program_id(1)
    c = CHUNK
    mm = functools.partial(_mm, precise=precise)

    @pl.when(j == 0)
    def _():
        if has_past:
            st_ref[0] = st0_ref[0]
            ct_ref[0] = ct0_ref[0]
            ns_ref[0] = ns0_ref[0]
            ms_ref[0] = ms0_ref[0]
            prev_ref[...] = cv0_ref[0]
        else:
            st_ref[...] = jnp.zeros(st_ref.shape, F32)
            ct_ref[...] = jnp.zeros(ct_ref.shape, F32)
            ns_ref[...] = jnp.zeros(ns_ref.shape, F32)
            ms_ref[...] = jnp.zeros(ms_ref.shape, F32)
            prev_ref[...] = jnp.zeros(prev_ref.shape, F32)

    def load(ref):
        x = ref[...]
        if rows < c:
            x = jnp.concatenate([x, jnp.zeros((c - rows, x.shape[1]), F32)], axis=0)
        return x

    row_i = lax.broadcasted_iota(jnp.int32, (c, 1), 0)
    grow = j * c + row_i
    valid = (grow >= lo_row) & (grow < hi_row)
    lane128 = lax.broadcasted_iota(jnp.int32, (1, LANE), 1)
    lane256 = lax.broadcasted_iota(jnp.int32, (1, 256), 1)
    t_i = lax.broadcasted_iota(jnp.int32, (c, c), 0)
    s_i = lax.broadcasted_iota(jnp.int32, (c, c), 1)
    causal = s_i <= t_i
    mats = mats_ref[...]
    tri = mats[:c]
    misc = load(misc_ref)

    la = jax.nn.log_sigmoid(mm(misc, wa_ref[...], NN_DIMS) + ba_ref[...]) * (1.0 / GLA_TAU)
    la = jnp.where(valid, la, 0.0)
    gq = load(gq_ref) * (GLA_DK ** -0.5)
    gk = jnp.where(valid, load(gk_ref), 0.0)
    gv = jnp.where(valid, load(gv_ref), 0.0)
    ex = _dot_exact_lhs(mats, la)
    b = ex[:c]
    ghead = [(lane128 >= GLA_DK * h) & (lane128 < GLA_DK * (h + 1)) for h in range(GLA_HEADS)]
    vhead = [(lane256 >= GLA_DV * h) & (lane256 < GLA_DV * (h + 1)) for h in range(GLA_HEADS)]
    tmod = lax.broadcasted_iota(jnp.int32, (GLA_HEADS * c, c), 0) % c
    smod = lax.broadcasted_iota(jnp.int32, (GLA_HEADS * c, c), 1)

    def expand(x):
        return jnp.concatenate([jnp.where(ghead[h], x, 0.0) for h in range(GLA_HEADS)], axis=0)

    a_all = jnp.where(tmod == smod, mm(expand(gq), gk, NT_DIMS), 0.0)
    for lv in range(N_LEVELS):
        e = jnp.exp(ex[(lv + 1) * c:(lv + 2) * c])
        bit = N_LEVELS - 1 - lv
        second = ((row_i >> bit) & 1) == 1
        qt = jnp.where(second, gq * e, 0.0)
        kt = jnp.where(second, 0.0, gk * e)
        a_lv = mm(expand(qt), kt, NT_DIMS)
        a_all = a_all + jnp.where((tmod >> (bit + 1)) == (smod >> (bit + 1)), a_lv, 0.0)
    o = jnp.zeros((c, 256), F32)
    for h in range(GLA_HEADS):
        o = o + jnp.where(vhead[h], mm(a_all[h * c:(h + 1) * c], gv, NN_DIMS), 0.0)
    st = st_ref[0]
    o = o + mm(gq * jnp.exp(b), st, NT_DIMS)
    bl = b[c - 1:c]
    kl = gk * jnp.exp(bl - b)
    bd_gla = ((lax.broadcasted_iota(jnp.int32, (256, LANE), 0) // GLA_DV)
              == (lax.broadcasted_iota(jnp.int32, (256, LANE), 1) // GLA_DK))
    st_ref[0] = st * jnp.exp(bl) + jnp.where(bd_gla, mm(gv, kl, TN_DIMS), 0.0)
    ms2 = jnp.zeros((c, 256), F32)
    for h in range(GLA_HEADS):
        s2 = jnp.sum(jnp.where(vhead[h], o * o, 0.0), axis=-1, keepdims=True) * (1.0 / GLA_DV)
        ms2 = ms2 + jnp.where(vhead[h], lax.rsqrt(s2 + RMS_EPS), 0.0)
    ogla = o * ms2 * gnorm_ref[...] * jax.nn.silu(load(gr_ref))
    ogla_ref[...] = ogla[:rows]

    mu = jnp.where(valid, load(mu_ref), 0.0)
    up = jnp.concatenate([prev_ref[...], mu], axis=0)
    cw = cw_ref[...]
    uc = cb_ref[...]
    for i in range(CONV_W):
        off = 8 - (CONV_W - 1) + i
        uc = uc + up[off:off + c] * cw[i:i + 1]
    uc = jax.nn.silu(uc)
    if rows == c:
        prev_ref[...] = mu[c - 8:]
    mq = mm(uc, wq_ref[...], NN_DIMS)
    mk = mm(uc, wk_ref[...], NN_DIMS) * (ML_DH ** -0.5)
    mv = load(mv_ref)
    g_raw = misc + gbias_ref[...]
    is_i = (lane128 >= MS_MI) & (lane128 < MS_MI + ML_HEADS)
    is_f = (lane128 >= MS_MF) & (lane128 < MS_MF + ML_HEADS)
    g = jnp.where(is_i, jnp.where(valid, g_raw, NEG_BIG),
                  jnp.where(is_f, jnp.where(valid, jax.nn.log_sigmoid(g_raw), 0.0), 0.0))
    fc_all = _dot_exact_lhs(tri, jnp.where(is_f, g, 0.0))
    gt = g.T
    ft = fc_all.T
    ms_old = ms_ref[0]
    ns_old = ns_ref[0][0:1]
    ct = ct_ref[0]
    inter = mm(mq, ct, NT_DIMS)
    qn = mq * ns_old
    hc = jnp.zeros((c, 256), F32)
    ws_full = jnp.zeros((c, 256), F32)
    wl_row = jnp.zeros((1, 256), F32)
    ms_new = jnp.zeros((1, LANE), F32)
    for h in range(ML_HEADS):
        fcol = fc_all[:, MS_MF + h:MS_MF + h + 1]
        frow = ft[MS_MF + h:MS_MF + h + 1]
        igrow = gt[MS_MI + h:MS_MI + h + 1]
        igcol = g[:, MS_MI + h:MS_MI + h + 1]
        m0 = ms_old[0:1, h:h + 1]
        d = jnp.where(causal, fcol - frow + igrow, NEG_BIG)
        a = fcol + m0
        m = jnp.maximum(a, jnp.max(d, axis=-1, keepdims=True))
        wp = jnp.exp(a - m)
        w = jnp.exp(d - m)
        qk = mm(jnp.where(vhead[h], mq, 0.0), mk, NT_DIMS) * w
        num = wp * inter + mm(qk, mv, NN_DIMS)
        den = wp * jnp.sum(jnp.where(vhead[h], qn, 0.0), axis=-1, keepdims=True) + jnp.sum(qk, axis=-1, keepdims=True)
        hh = num / jnp.maximum(jnp.abs(den), jnp.exp(-m))
        hc = hc + jnp.where(vhead[h], hh, 0.0)
        m_last = m[c - 1:c]
        f_last = fcol[c - 1:c]
        wl = jnp.exp(f_last + m0 - m_last)
        ws = jnp.exp(f_last - fcol + igcol - m_last)
        ws_full = ws_full + jnp.where(vhead[h], ws, 0.0)
        wl_row = wl_row + jnp.where(vhead[h], wl, 0.0)
        ms_new = ms_new + jnp.where(lane128 == h, m_last, 0.0)
    kw = mk * ws_full
    bd_ml = ((lax.broadcasted_iota(jnp.int32, (256, 256), 0) // ML_DH)
             == (lax.broadcasted_iota(jnp.int32, (256, 256), 1) // ML_DH))
    ct_ref[0] = ct * wl_row + jnp.where(bd_ml, mm(mv, kw, TN_DIMS), 0.0)
    ns_new = wl_row * ns_old + jnp.sum(kw, axis=0, keepdims=True)
    ns_ref[0] = jnp.broadcast_to(ns_new, (8, 256))
    ms_ref[0] = jnp.broadcast_to(ms_new, (8, LANE))
    hc = hc * jax.nn.sigmoid(load(mo_ref))
    hn = jnp.zeros((c, 256), F32)
    for h in range(ML_HEADS):
        mean = jnp.sum(jnp.where(vhead[h], hc, 0.0), axis=-1, keepdims=True) * (1.0 / ML_DH)
        xc = jnp.where(vhead[h], hc - mean, 0.0)
        var = jnp.sum(xc * xc, axis=-1, keepdims=True) * (1.0 / ML_DH)
        hn = hn + xc * lax.rsqrt(var + LN_EPS)
    oml = hn * mnorm_ref[...] + mskip_ref[...] * uc
    oml_ref[...] = oml[:rows]


def _rec(proj, lw, consts, n_seq, n_blocks, rows, lo_row, hi_row, past, precise):
    n = proj.shape[0]
    has_past = past is not None
    rmap = lambda cidx: (lambda s, j: (s * n_blocks + j, cidx))
    in_specs = [pl.BlockSpec((rows, LANE), rmap(PJ_GQ)), pl.BlockSpec((rows, LANE), rmap(PJ_GK)),
                pl.BlockSpec((rows, 256), rmap(PJ_GV)), pl.BlockSpec((rows, 256), rmap(PJ_GR)),
                pl.BlockSpec((rows, 256), rmap(PJ_MU)), pl.BlockSpec((rows, 256), rmap(PJ_MV)),
                pl.BlockSpec((rows, 256), rmap(PJ_MO)), pl.BlockSpec((rows, LANE), rmap(PJ_MISC))]
    args = [proj] * 8
    weights = [consts["mats"], lw["wa"], lw["ba"], lw["gnorm"], lw["conv_w"], lw["conv_b"], lw["wq_bd"], lw["wk_bd"],
               lw["gbias"], lw["mnorm"], lw["mskip"]]
    in_specs += [_const_spec(w.shape) for w in weights]
    args += weights
    smap = lambda s, j: (s, 0, 0)
    state_shapes = [(n_seq, 256, LANE), (n_seq, 256, 256), (n_seq, 8, 256), (n_seq, 8, LANE)]
    if has_past:
        in_specs += [pl.BlockSpec((1,) + sh[1:], smap) for sh in state_shapes] + [pl.BlockSpec((1, 8, 256), smap)]
        args += list(past)
    out_specs = [pl.BlockSpec((rows, 256), rmap(0)), pl.BlockSpec((rows, 256), rmap(0))]
    out_specs += [pl.BlockSpec((1,) + sh[1:], smap) for sh in state_shapes]
    out_shape = [jax.ShapeDtypeStruct((n, 256), F32), jax.ShapeDtypeStruct((n, 256), F32)]
    out_shape += [jax.ShapeDtypeStruct(sh, F32) for sh in state_shapes]
    return pl.pallas_call(
        functools.partial(_rec_kernel, rows=rows, lo_row=lo_row, hi_row=hi_row, has_past=has_past, precise=precise),
        grid=(n_seq, n_blocks),
        in_specs=in_specs,
        out_specs=out_specs,
        out_shape=out_shape,
        scratch_shapes=[pltpu.VMEM((8, 256), F32)],
        compiler_params=_params(("parallel", "arbitrary")),
        name="rec",
    )(*args)


def _post_kernel(x_ref, olat_ref, ogla_ref, oml_ref, wo_ref, wo2_ref, l1g_ref, l1b_ref, rw_ref, rb_ref,
                 wgu_ref, wd_ref, l2g_ref, l2b_ref, o_ref):
    tm = x_ref.shape[0]
    gm = jnp.concatenate([ogla_ref[...], oml_ref[...]], axis=1)
    gb = gm.astype(BF16)
    if wo2_ref.shape[0] == 3 * gm.shape[1]:
        gb = jnp.concatenate([gb, (gm - gb.astype(F32)).astype(BF16), gb], axis=1)
    y = _dot(olat_ref[...], wo_ref[...]) + _dot(gb, wo2_ref[...])
    x1 = _layer_norm(ALPHA * x_ref[...] + y, l1g_ref[...], l1b_ref[...])
    xb = x1.astype(BF16)

    x_mid = (x1 - xb.astype(F32)).astype(BF16)
    lg = _dot(jnp.concatenate([xb, x_mid, xb], axis=1), rw_ref[...])
    sc = jax.nn.sigmoid(lg).T[:N_EXPERTS]
    sel = sc + rb_ref[...]
    ng, ne = N_GROUPS, E_PER_GROUP
    a_sel = [sel[ng * i:ng * (i + 1)] for i in range(ne)]
    top2 = None
    for i in range(ne):
        for k in range(i + 1, ne):
            ps = a_sel[i] + a_sel[k]
            top2 = ps if top2 is None else jnp.maximum(top2, ps)
    tg = [top2[g:g + 1] for g in range(ng)]
    best = functools.reduce(jnp.maximum, tg)
    gsel, taken = [], jnp.zeros_like(best)
    for g in range(ng):
        hit = jnp.where(tg[g] >= best, 1.0, 0.0) * (1.0 - taken)
        gsel.append(hit)
        taken = taken + hit
    cs = [sum(gsel[g] * jnp.where(gsel[g] > 0.0, sel[ng * i + g:ng * i + g + 1], 0.0) for g in range(ng))
          for i in range(ne)]
    ss = [sum(gsel[g] * jnp.where(gsel[g] > 0.0, sc[ng * i + g:ng * i + g + 1], 0.0) for g in range(ng))
          for i in range(ne)]
    gates = []
    for i in range(ne):
        rank = jnp.zeros_like(best)
        for k in range(ne):
            if k < i:
                rank = rank + jnp.where(cs[k] >= cs[i], 1.0, 0.0)
            elif k > i:
                rank = rank + jnp.where(cs[k] > cs[i], 1.0, 0.0)
        gates.append(jnp.where(rank < 1.5, ss[i], 0.0))
    inv = 1.0 / sum(gates)
    rows = [gates[i] * inv * gsel[g] for i in range(ne) for g in range(ng)]
    wt = jnp.concatenate(rows + [jnp.zeros((LANE - N_EXPERTS, tm), F32)], axis=0)
    wcol = wt.T

    acc = jnp.zeros((tm, x_ref.shape[1]), F32)
    for e in range(N_EXPERTS):
        col = ng * (e % ne) + e // ne
        gu = _dot(xb, wgu_ref[e])
        hh = jax.nn.silu(gu[:, :D_EXPERT]) * gu[:, D_EXPERT:] * wcol[:, col:col + 1]
        acc = acc + _dot(hh.astype(BF16), wd_ref[e])
    o_ref[...] = _layer_norm(ALPHA * x1 + acc, l2g_ref[...], l2b_ref[...])


def _post(x, olat, ogla, oml, lw, rw, rb, tm, precise):
    n, d = x.shape
    row = lambda i: (i, 0)
    single = dict(pipeline_mode=pl.Buffered(1))
    consts = [lw["wo"], lw["wo2_3"] if precise else lw["wo2"], lw["ln1_g"], lw["ln1_b"], rw, rb, lw["wgu"], lw["wd"], lw["ln2_g"], lw["ln2_b"]]
    in_specs = [pl.BlockSpec((tm, d), row), pl.BlockSpec((tm, 1024), row), pl.BlockSpec((tm, 256), row),
                pl.BlockSpec((tm, 256), row)]
    for w in consts:
        nd = w.ndim
        in_specs.append(pl.BlockSpec(w.shape, (lambda i, _nd=nd: (0,) * _nd), **single))
    return pl.pallas_call(
        _post_kernel,
        grid=(n // tm,),
        in_specs=in_specs,
        out_specs=pl.BlockSpec((tm, d), row),
        out_shape=jax.ShapeDtypeStruct((n, d), F32),
        compiler_params=_params(("parallel",)),
        name="post",
    )(x, olat, ogla, oml, *consts)


def _rope_tables(pos):
    half = MLA_ROPE // 2
    inv = ROPE_THETA ** (-jnp.arange(half, dtype=F32) / half)
    ang = pos.astype(F32)[:, None] * inv
    cos, sin = jnp.cos(ang), jnp.sin(ang)
    c2 = jnp.concatenate([cos, cos], axis=1)
    s2 = jnp.concatenate([-sin, sin], axis=1)
    cq = jnp.tile(c2, (1, MLA_HEADS))
    sq = jnp.tile(s2, (1, MLA_HEADS))
    kt = jnp.concatenate([c2, s2, jnp.zeros((pos.shape[0], LANE - 2 * MLA_ROPE), F32)], axis=1)
    return cq, sq, kt


def _prep_layer(l, p):
    d = p["w_in"].shape[1]
    offs = np.concatenate([[0], np.cumsum(IN_SIZES)])
    wi = p["w_in"][l]
    cq, ckv, kr, gq, gk, gv, ga, gr, mu, mv, mi, mf, mo = [wi[:, offs[i]:offs[i + 1]] for i in range(13)]
    half = MLA_ROPE // 2
    kr_sw = jnp.concatenate([kr[:, half:], kr[:, :half]], axis=1)
    misc = jnp.concatenate([kr, kr_sw, ga, mi, mf, jnp.zeros((d, LANE - MS_MF - ML_HEADS), F32)], axis=1)
    w_in32 = jnp.concatenate([cq, gv, gr, mu, mv, mo, ckv, gq, gk, misc], axis=1)
    w_in = w_in32.astype(BF16)
    w_in3 = _split_weight(w_in32) if l == 0 else None

    dq = MLA_NOPE + MLA_ROPE
    wuq = p["mla_w_uq"][l].reshape(MLA_Q_RANK, MLA_HEADS, dq)
    wukv = p["mla_w_ukv"][l].reshape(MLA_KV_RANK, MLA_HEADS, MLA_NOPE + MLA_V)
    wuk_t = jnp.transpose(wukv[..., :MLA_NOPE], (1, 2, 0))
    wql = _fold(jnp.transpose(wuq[..., :MLA_NOPE], (1, 0, 2)), wuk_t)
    wql = jnp.transpose(wql, (1, 0, 2)).reshape(MLA_Q_RANK, MLA_HEADS * MLA_KV_RANK)
    wqr = wuq[..., MLA_NOPE:]
    wqr_sw = jnp.concatenate([wqr[..., half:], wqr[..., :half]], axis=-1)
    wq = jnp.concatenate([wql, wqr.reshape(MLA_Q_RANK, -1), wqr_sw.reshape(MLA_Q_RANK, -1)], axis=1).astype(BF16)

    wout = p["w_out"][l]
    nv = MLA_HEADS * MLA_V
    wuv = jnp.transpose(wukv[..., MLA_NOPE:], (1, 0, 2))
    wo_mla = _fold(wuv, wout[:nv].reshape(MLA_HEADS, MLA_V, d)).reshape(MLA_HEADS * MLA_KV_RANK, d)
    wo = wo_mla.astype(BF16)
    wo2 = wout[nv:].astype(BF16)
    wo2_3 = _split_weight(wout[nv:]) if l == 0 else None

    wa = jnp.zeros((LANE, LANE), F32).at[MS_GA:MS_GA + GLA_GATE_RANK].set(p["gla_w_a"][l])
    gbias = (jnp.zeros((1, LANE), F32).at[0, MS_MI:MS_MI + ML_HEADS].set(p["ml_b_i"][l])
             .at[0, MS_MF:MS_MF + ML_HEADS].set(p["ml_b_f"][l]))

    def block_diag(w):
        out = jnp.zeros((ML_WIDTH, ML_WIDTH), F32)
        for h in range(ML_HEADS):
            out = out.at[h * ML_DH:(h + 1) * ML_DH, h * ML_DH:(h + 1) * ML_DH].set(w[h])
        return out

    conv_w = jnp.concatenate([p["ml_conv_w"][l], jnp.zeros((8 - CONV_W, ML_WIDTH), F32)], axis=0)
    wgu = jnp.concatenate([p["moe_w_gate"][l], p["moe_w_up"][l]], axis=2).astype(BF16)
    return dict(
        w_in=w_in, w_in3=w_in3, wq=wq, wo=wo, wo2=wo2, wo2_3=wo2_3, wa=wa, gbias=gbias,
        qn=p["mla_q_norm"][l][None], kvn=p["mla_kv_norm"][l][None],
        ba=p["gla_b_a"][l][None], gnorm=jnp.tile(p["gla_norm"][l], GLA_HEADS)[None],
        conv_w=conv_w, conv_b=p["ml_conv_b"][l][None], wq_bd=block_diag(p["ml_w_q"][l]),
        wk_bd=block_diag(p["ml_w_k"][l]), mnorm=p["ml_norm"][l][None], mskip=p["ml_skip"][l][None],
        ln1_g=p["ln1_g"][l][None], ln1_b=p["ln1_b"][l][None], ln2_g=p["ln2_g"][l][None], ln2_b=p["ln2_b"][l][None],
        wgu=wgu, wd=p["moe_w_down"][l].astype(BF16),
    )


def _gla_state_out(st):
    n = st.shape[0]
    s5 = st.reshape(n, GLA_HEADS, GLA_DV, GLA_HEADS, GLA_DK)
    return jnp.stack([jnp.transpose(s5[:, h, :, h, :], (0, 2, 1)) for h in range(GLA_HEADS)], axis=1)


def _gla_state_in(s):
    n = s.shape[0]
    out = jnp.zeros((n, GLA_HEADS, GLA_DV, GLA_HEADS, GLA_DK), F32)
    for h in range(GLA_HEADS):
        out = out.at[:, h, :, h, :].set(jnp.transpose(s[:, h], (0, 2, 1)))
    return out.reshape(n, GLA_HEADS * GLA_DV, GLA_HEADS * GLA_DK)


def _ml_state_out(ct):
    n = ct.shape[0]
    c5 = ct.reshape(n, ML_HEADS, ML_DH, ML_HEADS, ML_DH)
    return jnp.stack([jnp.transpose(c5[:, h, :, h, :], (0, 2, 1)) for h in range(ML_HEADS)], axis=1)


def _ml_state_in(cs):
    n = cs.shape[0]
    out = jnp.zeros((n, ML_HEADS, ML_DH, ML_HEADS, ML_DH), F32)
    for h in range(ML_HEADS):
        out = out.at[:, h, :, h, :].set(jnp.transpose(cs[:, h], (0, 2, 1)))
    return out.reshape(n, ML_WIDTH, ML_WIDTH)


def _pick_tile(n, cands):
    for t in cands:
        if n % t == 0:
            return t
    raise ValueError(f"no row tile for {n} rows")


def kernel(x_prompt, x_sample, cache_kv_latent, cache_k_rope, state_gla, state_mlstm_C, state_mlstm_n, state_mlstm_m, state_mlstm_conv, page_table, meta_tokens, ln_in_g, ln_in_b, w_in, mla_q_norm, mla_w_uq, mla_kv_norm, mla_w_ukv, gla_w_a, gla_b_a, gla_norm, ml_conv_w, ml_conv_b, ml_w_q, ml_w_k, ml_b_i, ml_b_f, ml_norm, ml_skip, w_out, ln1_g, ln1_b, ln2_g, ln2_b, router_w, router_bias, moe_w_gate, moe_w_up, moe_w_down):
    p = dict(w_in=w_in, mla_q_norm=mla_q_norm, mla_w_uq=mla_w_uq, mla_kv_norm=mla_kv_norm, mla_w_ukv=mla_w_ukv,
             gla_w_a=gla_w_a, gla_b_a=gla_b_a, gla_norm=gla_norm, ml_conv_w=ml_conv_w, ml_conv_b=ml_conv_b,
             ml_w_q=ml_w_q, ml_w_k=ml_w_k, ml_b_i=ml_b_i, ml_b_f=ml_b_f, ml_norm=ml_norm, ml_skip=ml_skip,
             w_out=w_out, ln1_g=ln1_g, ln1_b=ln1_b, ln2_g=ln2_g, ln2_b=ln2_b, moe_w_gate=moe_w_gate,
             moe_w_up=moe_w_up, moe_w_down=moe_w_down)
    depth = w_in.shape[0]
    nb, seq, d = x_prompt.shape
    ns, t_new, _ = x_sample.shape
    n_pages = page_table.shape[1]
    page = cache_kv_latent.shape[2]
    past_len = n_pages * page
    lp = PAD + seq
    assert seq % CHUNK == 0 and t_new == 8 and t_new >= CONV_W - 1

    layers = [_prep_layer(l, p) for l in range(depth)]
    consts = dict(mats=jnp.asarray(_level_mats(), BF16))
    perm = np.array([E_PER_GROUP * g + i for i in range(E_PER_GROUP) for g in range(N_GROUPS)])
    rw_hi, rw_mid = _split2_bits(router_w.astype(F32)[:, perm])
    rw = jnp.pad(jnp.concatenate([rw_hi, rw_hi, rw_mid], axis=0), ((0, 0), (0, LANE - N_EXPERTS)))
    rb = router_bias.astype(F32)[perm][:, None]
    g_in, b_in = ln_in_g[None], ln_in_b[None]

    meta = jnp.broadcast_to(meta_tokens[None].astype(F32), (nb, N_META, d))
    xp = jnp.concatenate([jnp.zeros((nb, PAD - N_META, d), F32), meta, x_prompt], axis=1).reshape(nb * lp, d)
    tabs_p = _rope_tables(jnp.arange(lp) - (PAD - N_META))
    tm_p = _pick_tile(nb * lp, (512, 384, 256, 128))
    tk_p = _pick_tile(lp, (384, 128))
    p_out = dict(kv=[], kr=[], gla=[], C=[], n=[], m=[], conv=[])
    for l in range(depth):
        lw = layers[l]
        precise = l == 0
        proj, xp = _in_proj(xp, g_in, b_in, lw["w_in3"] if precise else lw["w_in"], l == 0, tm_p)
        qlat, qrope, ckvn, krope, kcat, vt = _mla_prep(proj, tabs_p, lw["qn"], lw["kvn"], lw["wq"], tk_p, lp // tk_p)
        olat = _attn_prompt(qlat, qrope, kcat, vt, nb, lp, CHUNK, tk_p)
        ogla, oml, st, ct, nst, mst = _rec(proj, lw, consts, nb, lp // CHUNK, CHUNK, PAD - N_META, lp, None, precise)
        xp = _post(xp, olat, ogla, oml, lw, rw, rb, tm_p, precise)
        lo = PAD - N_META
        p_out["kv"].append(ckvn.reshape(nb, lp, -1)[:, lo:])
        p_out["kr"].append(krope.reshape(nb, lp, -1)[:, lo:])
        p_out["gla"].append(_gla_state_out(st))
        p_out["C"].append(_ml_state_out(ct))
        p_out["n"].append(nst[:, 0].reshape(nb, ML_HEADS, ML_DH))
        p_out["m"].append(mst[:, 0, :ML_HEADS])
        mu_cols = proj[:, PJ_MU * 256:(PJ_MU + 1) * 256].reshape(nb, lp, ML_WIDTH)
        p_out["conv"].append(mu_cols[:, lp - (CONV_W - 1):])
    y_prompt = xp.reshape(nb, lp, d)[:, PAD:]

    xs = x_sample.reshape(ns * t_new, d)
    tabs_s = _rope_tables(past_len + jnp.arange(t_new))
    tm_s = _pick_tile(ns * t_new, (512, 256, 128, 64, 32, 16))
    tabs_s = tuple(jnp.tile(t, (tm_s // t_new, 1)) for t in tabs_s)
    cache_rt = jnp.swapaxes(cache_k_rope, 2, 3)
    s_out = dict(kv=[], kr=[], gla=[], C=[], n=[], m=[], conv=[])
    for l in range(depth):
        lw = layers[l]
        proj, xs = _in_proj(xs, g_in, b_in, lw["w_in"], l == 0, tm_s)
        qlat, qrope, ckvn, krope, kcat, _ = _mla_prep(proj, tabs_s, lw["qn"], lw["kvn"], lw["wq"], tm_s, 1)
        olat = _attn_sample(qlat, qrope, kcat, cache_kv_latent, cache_rt, page_table, t_new, l)
        ns0 = jnp.broadcast_to(state_mlstm_n[l].reshape(ns, 1, ML_WIDTH), (ns, 8, ML_WIDTH))
        ms0 = jnp.broadcast_to(jnp.pad(state_mlstm_m[l], ((0, 0), (0, LANE - ML_HEADS)))[:, None], (ns, 8, LANE))
        cv0 = jnp.pad(state_mlstm_conv[l], ((0, 0), (8 - (CONV_W - 1), 0), (0, 0)))
        past = (_gla_state_in(state_gla[l]), _ml_state_in(state_mlstm_C[l]), ns0, ms0, cv0)
        ogla, oml, st, ct, nst, mst = _rec(proj, lw, consts, ns, 1, t_new, 0, t_new, past, False)
        xs = _post(xs, olat, ogla, oml, lw, rw, rb, tm_s, False)
        s_out["kv"].append(ckvn.reshape(ns, t_new, -1))
        s_out["kr"].append(krope.reshape(ns, t_new, -1))
        s_out["gla"].append(_gla_state_out(st))
        s_out["C"].append(_ml_state_out(ct))
        s_out["n"].append(nst[:, 0].reshape(ns, ML_HEADS, ML_DH))
        s_out["m"].append(mst[:, 0, :ML_HEADS])
        mu_cols = proj[:, PJ_MU * 256:(PJ_MU + 1) * 256].reshape(ns, t_new, ML_WIDTH)
        s_out["conv"].append(mu_cols[:, t_new - (CONV_W - 1):])
    y_sample = xs.reshape(ns, t_new, d)

    st_p = [jnp.stack(p_out[k]) for k in ("kv", "kr", "gla", "C", "n", "m", "conv")]
    st_s = [jnp.stack(s_out[k]) for k in ("kv", "kr", "gla", "C", "n", "m", "conv")]
    return (y_prompt, y_sample, *st_p, *st_s)
```

```python
import functools

import numpy as np
import jax
import jax.numpy as jnp
from jax import lax
from jax.experimental import pallas as pl
from jax.experimental.pallas import tpu as pltpu

F32 = jnp.float32
BF16 = jnp.bfloat16

N_META = 16
MLA_HEADS = 8
MLA_Q_RANK = 256
MLA_KV_RANK = 128
MLA_NOPE = 64
MLA_ROPE = 32
MLA_V = 64
MLA_SCALE = (MLA_NOPE + MLA_ROPE) ** -0.5
ROPE_THETA = 10000.0
GLA_HEADS = 4
GLA_DK = 32
GLA_DV = 64
GLA_GATE_RANK = 16
GLA_TAU = 16.0
ML_HEADS = 4
ML_DH = 64
ML_WIDTH = ML_HEADS * ML_DH
CONV_W = 4
N_EXPERTS = 16
N_GROUPS = 4
E_PER_GROUP = N_EXPERTS // N_GROUPS
D_EXPERT = 256
DEPTH = 2
ALPHA = (2 * DEPTH) ** 0.25
LN_EPS = 1e-5
RMS_EPS = 1e-6
IN_SIZES = (MLA_Q_RANK, MLA_KV_RANK, MLA_ROPE,
            GLA_HEADS * GLA_DK, GLA_HEADS * GLA_DK, GLA_HEADS * GLA_DV, GLA_GATE_RANK, GLA_HEADS * GLA_DV,
            ML_WIDTH, ML_WIDTH, ML_HEADS, ML_HEADS, ML_WIDTH)

LANE = 128
CHUNK = 128
PAD = 128
D_PROJ = 2048
PJ_CQ, PJ_GV, PJ_GR, PJ_MU, PJ_MV, PJ_MO = 0, 1, 2, 3, 4, 5
PJ_CKV, PJ_GQ, PJ_GK, PJ_MISC = 12, 13, 14, 15
MS_KR, MS_KRSW, MS_GA, MS_MI, MS_MF = 0, 32, 64, 80, 84
NEG_BIG = -1e30
VMEM_LIMIT = 56 * 1024 * 1024

NT_DIMS = (((1,), (1,)), ((), ()))
TN_DIMS = (((0,), (0,)), ((), ()))


def _dot(a, b):
    return jnp.dot(a, b, preferred_element_type=F32)


def _dot_nt(a, b):
    return lax.dot_general(a, b, NT_DIMS, preferred_element_type=F32)


def _dot_tn(a, b):
    return lax.dot_general(a, b, TN_DIMS, preferred_element_type=F32)


NN_DIMS = (((1,), (0,)), ((), ()))


def _mm(a, b, dims, precise):
    ah, bh = a.astype(BF16), b.astype(BF16)
    out = lax.dot_general(ah, bh, dims, preferred_element_type=F32)
    if precise:
        am = (a - ah.astype(F32)).astype(BF16)
        bm = (b - bh.astype(F32)).astype(BF16)
        out = (out + lax.dot_general(am, bh, dims, preferred_element_type=F32)
               + lax.dot_general(ah, bm, dims, preferred_element_type=F32))
    return out


def _split3(x):
    hi = x.astype(BF16)
    r = x - hi.astype(F32)
    mid = r.astype(BF16)
    lo = (r - mid.astype(F32)).astype(BF16)
    return hi, mid, lo


def _dot_exact_lhs(m01, x):
    hi, mid, lo = _split3(x)
    return _dot(m01, hi) + _dot(m01, mid) + _dot(m01, lo)


def _layer_norm(x, g, b):
    mu = jnp.mean(x, axis=-1, keepdims=True)
    xc = x - mu
    var = jnp.mean(xc * xc, axis=-1, keepdims=True)
    return xc * lax.rsqrt(var + LN_EPS) * g + b


def _params(sem):
    return pltpu.CompilerParams(dimension_semantics=sem, vmem_limit_bytes=VMEM_LIMIT)


def _const_spec(shape):
    nd = len(shape)
    return pl.BlockSpec(shape, lambda *_: (0,) * nd)


def _fold_kernel(a_ref, b_ref, o_ref):
    o_ref[0] = jnp.dot(a_ref[0], b_ref[0], preferred_element_type=F32, precision=lax.Precision.HIGHEST)


def _fold(a, b):
    h, m, k = a.shape
    n = b.shape[2]
    return pl.pallas_call(
        _fold_kernel,
        grid=(h,),
        in_specs=[pl.BlockSpec((1, m, k), lambda i: (i, 0, 0)), pl.BlockSpec((1, k, n), lambda i: (i, 0, 0))],
        out_specs=pl.BlockSpec((1, m, n), lambda i: (i, 0, 0)),
        out_shape=jax.ShapeDtypeStruct((h, m, n), F32),
        compiler_params=_params(("arbitrary",)),
        name="fold",
    )(a, b)


def _in_proj_kernel(x_ref, g_ref, b_ref, w_ref, proj_ref, *xn_refs, ln_in):
    x = x_ref[...]
    if ln_in:
        x = _layer_norm(x, g_ref[...], b_ref[...])
        xn_refs[0][...] = x
    xb = x.astype(BF16)
    if w_ref.shape[0] == 3 * x.shape[1]:
        x_mid = (x - xb.astype(F32)).astype(BF16)
        xb = jnp.concatenate([xb, x_mid, xb], axis=1)
    proj_ref[...] = _dot(xb, w_ref[...])


def _split2_bits(w):
    bits = lax.bitcast_convert_type(w.astype(F32), jnp.uint32) & jnp.uint32(0xFFFF0000)
    hi = lax.bitcast_convert_type(bits, F32)
    return hi.astype(BF16), (w - hi).astype(BF16)


def _split_weight(w):
    hi, mid = _split2_bits(w)
    return jnp.concatenate([hi, hi, mid], axis=0)


def _in_proj(x, g, b, w, ln_in, tm):
    n, d = x.shape
    out_shape = [jax.ShapeDtypeStruct((n, D_PROJ), F32)]
    out_specs = [pl.BlockSpec((tm, D_PROJ), lambda i: (i, 0))]
    if ln_in:
        out_shape.append(jax.ShapeDtypeStruct((n, d), F32))
        out_specs.append(pl.BlockSpec((tm, d), lambda i: (i, 0)))
    res = pl.pallas_call(
        functools.partial(_in_proj_kernel, ln_in=ln_in),
        grid=(n // tm,),
        in_specs=[pl.BlockSpec((tm, d), lambda i: (i, 0)), _const_spec((1, d)), _const_spec((1, d)),
                  pl.BlockSpec(w.shape, lambda i: (0, 0), pipeline_mode=pl.Buffered(1))],
        out_specs=out_specs,
        out_shape=out_shape,
        compiler_params=_params(("parallel",)),
        name="in_proj",
    )(x, g, b, w)
    return (res[0], res[1]) if ln_in else (res[0], x)


def _mla_prep_kernel(cq_ref, ckv_ref, misc_ref, cq_tab, sq_tab, k_tab, qn_ref, kvn_ref, wq_ref,
                     qlat_ref, qrope_ref, ckvn_ref, krope_ref, kcat_ref, vt_ref):
    cq = cq_ref[...]
    cqn = cq * lax.rsqrt(jnp.mean(cq * cq, axis=-1, keepdims=True) + RMS_EPS) * qn_ref[...]
    q = _dot(cqn.astype(BF16), wq_ref[...])
    nl = MLA_HEADS * MLA_KV_RANK
    nr = MLA_HEADS * MLA_ROPE
    qlat_ref[...] = (q[:, :nl] * MLA_SCALE).astype(BF16)
    rot = q[:, nl:nl + nr] * cq_tab[...] + q[:, nl + nr:] * sq_tab[...]
    qrope_ref[...] = (rot * MLA_SCALE).astype(BF16)
    ckv = ckv_ref[...]
    ckvn = ckv * lax.rsqrt(jnp.mean(ckv * ckv, axis=-1, keepdims=True) + RMS_EPS) * kvn_ref[...]
    ckvn_ref[...] = ckvn
    p = misc_ref[...] * k_tab[...]
    kr = p[:, MS_KR:MS_KR + MLA_ROPE] + p[:, MS_KRSW:MS_KRSW + MLA_ROPE]
    krope_ref[...] = kr
    kcat_ref[...] = jnp.concatenate([ckvn, kr], axis=1).astype(BF16)
    vt_ref[0] = ckvn.T.astype(BF16)


def _mla_prep(proj, tabs, qn, kvn, wq, tp, tab_blocks):
    n = proj.shape[0]
    cq_tab, sq_tab, k_tab = tabs
    nq = wq.shape[1]
    dk = MLA_KV_RANK + MLA_ROPE
    row = lambda c: (lambda i: (i, c))
    tab = lambda i: (i % tab_blocks, 0)
    return pl.pallas_call(
        _mla_prep_kernel,
        grid=(n // tp,),
        in_specs=[pl.BlockSpec((tp, 256), row(PJ_CQ)), pl.BlockSpec((tp, LANE), row(PJ_CKV)),
                  pl.BlockSpec((tp, LANE), row(PJ_MISC)),
                  pl.BlockSpec((tp, 256), tab), pl.BlockSpec((tp, 256), tab), pl.BlockSpec((tp, LANE), tab),
                  _const_spec((1, MLA_Q_RANK)), _const_spec((1, MLA_KV_RANK)), _const_spec((MLA_Q_RANK, nq))],
        out_specs=[pl.BlockSpec((tp, 1024), row(0)), pl.BlockSpec((tp, 256), row(0)),
                   pl.BlockSpec((tp, MLA_KV_RANK), row(0)), pl.BlockSpec((tp, MLA_ROPE), row(0)),
                   pl.BlockSpec((tp, dk), row(0)), pl.BlockSpec((1, MLA_KV_RANK, tp), lambda i: (i, 0, 0))],
        out_shape=[jax.ShapeDtypeStruct((n, 1024), BF16), jax.ShapeDtypeStruct((n, 256), BF16),
                   jax.ShapeDtypeStruct((n, MLA_KV_RANK), F32), jax.ShapeDtypeStruct((n, MLA_ROPE), F32),
                   jax.ShapeDtypeStruct((n, dk), BF16), jax.ShapeDtypeStruct((n // tp, MLA_KV_RANK, tp), BF16)],
        compiler_params=_params(("parallel",)),
        name="mla_prep",
    )(proj, proj, proj, cq_tab, sq_tab, k_tab, qn, kvn, wq)


def _attn_prompt_kernel(ql_ref, qr_ref, k_ref, vt_ref, o_ref, m_ref, l_ref, acc_ref, sa_ref, sb_ref, *, tq, tk,
                        lo_key):
    qi = pl.program_id(1)
    m_rows = tq * MLA_HEADS
    ql = ql_ref[...]
    qr = qr_ref[...]
    q = jnp.concatenate(
        [jnp.concatenate([ql[:, h * MLA_KV_RANK:(h + 1) * MLA_KV_RANK], qr[:, h * MLA_ROPE:(h + 1) * MLA_ROPE]], axis=1)
         for h in range(MLA_HEADS)], axis=0)
    qpos = qi * tq + lax.broadcasted_iota(jnp.int32, (1, m_rows), 1) % tq
    lowk = jnp.where(qpos < lo_key, 0, lo_key)

    def scores(j):
        start = j * tk if isinstance(j, int) else pl.multiple_of(j * tk, tk)
        return _dot_nt(k_ref[pl.ds(start, tk), :], q)

    def consume(s, j, masked, first):
        if masked:
            kpos = j * tk + lax.broadcasted_iota(jnp.int32, (tk, 1), 0)
            s = jnp.where(kpos <= qpos, jnp.where(kpos >= lowk, s, NEG_BIG), NEG_BIG)
        smax = jnp.max(s, axis=0, keepdims=True)
        if first:
            m_new = smax
            p = jnp.exp(s - m_new)
            l_ref[...] = jnp.sum(p, axis=0, keepdims=True)
            acc_ref[...] = _dot(vt_ref[j], p.astype(BF16))
        else:
            m_old = m_ref[...]
            m_new = jnp.maximum(m_old, smax)
            a = jnp.exp(m_old - m_new)
            p = jnp.exp(s - m_new)
            l_ref[...] = l_ref[...] * a + jnp.sum(p, axis=0, keepdims=True)
            acc_ref[...] = acc_ref[...] * a + _dot(vt_ref[j], p.astype(BF16))
        m_ref[...] = m_new

    jd = (qi * tq) // tk
    consume(scores(0), 0, True, True)

    n_mid = jnp.maximum(jd - 1, 0)
    nk = vt_ref.shape[0]

    @pl.when(n_mid > 0)
    def _():
        sa_ref[...] = scores(jnp.minimum(1, nk - 1))

    def pair(i, c):
        t1 = 1 + 2 * i
        sb_ref[...] = scores(t1 + 1)
        consume(sa_ref[...], t1, False, False)
        sa_ref[...] = scores(jnp.minimum(t1 + 2, nk - 1))
        consume(sb_ref[...], t1 + 1, False, False)
        return c

    lax.fori_loop(0, n_mid // 2, pair, 0)

    @pl.when(n_mid % 2 == 1)
    def _():
        consume(sa_ref[...], jd - 1, False, False)

    @pl.when(jd > 0)
    def _():
        consume(scores(jd), jd, True, False)

    o = acc_ref[...] / l_ref[...]
    for h in range(MLA_HEADS):
        o_ref[:, h * MLA_KV_RANK:(h + 1) * MLA_KV_RANK] = o[:, h * tq:(h + 1) * tq].T.astype(BF16)


def _attn_prompt(qlat, qrope, kcat, vt, n_seq, lp, tq, tk):
    m_rows = tq * MLA_HEADS
    nq = lp // tq
    nk = lp // tk
    dk = MLA_KV_RANK + MLA_ROPE
    qmap = lambda b, i: (b * nq + i, 0)
    return pl.pallas_call(
        functools.partial(_attn_prompt_kernel, tq=tq, tk=tk, lo_key=PAD - N_META),
        grid=(n_seq, nq),
        in_specs=[pl.BlockSpec((tq, MLA_HEADS * MLA_KV_RANK), qmap), pl.BlockSpec((tq, MLA_HEADS * MLA_ROPE), qmap),
                  pl.BlockSpec((lp, dk), lambda b, i: (b, 0)),
                  pl.BlockSpec((nk, MLA_KV_RANK, tk), lambda b, i: (b, 0, 0))],
        out_specs=pl.BlockSpec((tq, MLA_HEADS * MLA_KV_RANK), qmap),
        out_shape=jax.ShapeDtypeStruct(qlat.shape, BF16),
        scratch_shapes=[pltpu.VMEM((1, m_rows), F32), pltpu.VMEM((1, m_rows), F32),
                        pltpu.VMEM((MLA_KV_RANK, m_rows), F32),
                        pltpu.VMEM((tk, m_rows), F32), pltpu.VMEM((tk, m_rows), F32)],
        compiler_params=_params(("parallel", "arbitrary")),
        name="attn_prompt",
    )(qlat, qrope, kcat, vt)


def _attn_sample_kernel(pt_ref, ql_ref, qr_ref, knew_ref, cache_c, cache_r, o_ref,
                        bufc, bufr, semc, semr, m_ref, l_ref, acc_ref, *, n_pages, gp, t_new, page, layer):
    s_id = pl.program_id(0)
    n_seq = pl.num_programs(0)
    ng = n_pages // gp
    gk = gp * page

    def copies(seq, g, slot):
        out = []
        for p in range(gp):
            pg = pt_ref[seq * n_pages + g * gp + p]
            out.append(pltpu.make_async_copy(cache_c.at[layer, pg], bufc.at[slot, pl.ds(p * page, page)],
                                             semc.at[slot]))
            out.append(pltpu.make_async_copy(cache_r.at[layer, pg], bufr.at[slot, p], semr.at[slot]))
        return out

    @pl.when(s_id == 0)
    def _():
        for c in copies(0, 0, 0):
            c.start()

    ql = ql_ref[...]
    qr = qr_ref[...]
    q = jnp.concatenate([ql, qr], axis=1)
    m_ref[...] = jnp.full(m_ref.shape, NEG_BIG, F32)
    l_ref[...] = jnp.zeros(l_ref.shape, F32)
    acc_ref[...] = jnp.zeros(acc_ref.shape, F32)

    def update(s, v):
        m_old = m_ref[...]
        m_new = jnp.maximum(m_old, jnp.max(s, axis=-1, keepdims=True))
        a = jnp.exp(m_old - m_new)
        p = jnp.exp(s - m_new)
        l_ref[...] = l_ref[...] * a + jnp.sum(p, axis=-1, keepdims=True)
        acc_ref[...] = acc_ref[...] * a + _dot(p.astype(BF16), v)
        m_ref[...] = m_new

    def group(g, c):
        it = s_id * ng + g
        slot = it % 2
        nxt_seq = jnp.where(g + 1 < ng, s_id, s_id + 1)
        nxt_g = jnp.where(g + 1 < ng, g + 1, 0)

        @pl.when(nxt_seq < n_seq)
        def _():
            for cp in copies(nxt_seq, nxt_g, 1 - slot):
                cp.start()

        for cp in copies(s_id, g, slot):
            cp.wait()
        kc = bufc[slot].astype(BF16)
        krt = jnp.concatenate([bufr[slot, p] for p in range(gp)], axis=1).astype(BF16)
        update(_dot_nt(ql, kc) + _dot(qr, krt), kc)
        return c

    lax.fori_loop(0, ng, group, 0)

    kn = knew_ref[...]
    s = _dot_nt(q, kn)
    rows = q.shape[0]
    qt = lax.broadcasted_iota(jnp.int32, (rows, t_new), 0) // MLA_HEADS
    kt = lax.broadcasted_iota(jnp.int32, (rows, t_new), 1)
    s = jnp.where(kt <= qt, s, NEG_BIG)
    update(s, kn[:, :MLA_KV_RANK])
    o_ref[...] = (acc_ref[...] / l_ref[...]).astype(BF16)


def _attn_sample(qlat, qrope, kcat, cache_c, cache_r, page_table, t_new, layer):
    n_seq, n_pages = page_table.shape
    page = cache_c.shape[2]
    gp = _pick_tile(n_pages, (64, 32, 16, 8, 4, 2, 1))
    rows = t_new * MLA_HEADS
    dk = MLA_KV_RANK + MLA_ROPE
    ql = qlat.reshape(-1, MLA_KV_RANK)
    qr = qrope.reshape(-1, MLA_ROPE)
    grid_spec = pltpu.PrefetchScalarGridSpec(
        num_scalar_prefetch=1,
        grid=(n_seq,),
        in_specs=[pl.BlockSpec((rows, MLA_KV_RANK), lambda s, pt: (s, 0)),
                  pl.BlockSpec((rows, MLA_ROPE), lambda s, pt: (s, 0)),
                  pl.BlockSpec((t_new, dk), lambda s, pt: (s, 0)),
                  pl.BlockSpec(memory_space=pl.ANY), pl.BlockSpec(memory_space=pl.ANY)],
        out_specs=pl.BlockSpec((rows, MLA_KV_RANK), lambda s, pt: (s, 0)),
        scratch_shapes=[pltpu.VMEM((2, gp * page, MLA_KV_RANK), F32), pltpu.VMEM((2, gp, MLA_ROPE, page), F32),
                        pltpu.SemaphoreType.DMA((2,)), pltpu.SemaphoreType.DMA((2,)),
                        pltpu.VMEM((rows, 1), F32), pltpu.VMEM((rows, 1), F32),
                        pltpu.VMEM((rows, MLA_KV_RANK), F32)],
    )
    out = pl.pallas_call(
        functools.partial(_attn_sample_kernel, n_pages=n_pages, gp=gp, t_new=t_new, page=page, layer=layer),
        grid_spec=grid_spec,
        out_shape=jax.ShapeDtypeStruct(ql.shape, BF16),
        compiler_params=_params(("arbitrary",)),
        name="attn_sample",
    )(page_table.reshape(-1), ql, qr, kcat, cache_c, cache_r)
    return out.reshape(-1, MLA_HEADS * MLA_KV_RANK)


def _level_mats():
    c = CHUNK
    t = np.arange(c)[:, None]
    s = np.arange(c)[None, :]
    mats = [(s <= t)]
    h = c // 2
    while h >= 1:
        start = (t // (2 * h)) * (2 * h) + h
        second = (t % (2 * h)) >= h
        mats.append(np.where(second, (s >= start) & (s <= t), (s > t) & (s < start)))
        h //= 2
    return np.concatenate(mats, axis=0).astype(np.float32)


N_LEVELS = int(np.log2(CHUNK))


def _rec_kernel(*refs, rows, lo_row, hi_row, has_past, precise):
    (gq_ref, gk_ref, gv_ref, gr_ref, mu_ref, mv_ref, mo_ref, misc_ref,
     mats_ref, wa_ref, ba_ref, gnorm_ref, cw_ref, cb_ref, wq_ref, wk_ref, gbias_ref, mnorm_ref, mskip_ref) = refs[:19]
    k = 19
    if has_past:
        st0_ref, ct0_ref, ns0_ref, ms0_ref, cv0_ref = refs[k:k + 5]
        k += 5
    ogla_ref, oml_ref, st_ref, ct_ref, ns_ref, ms_ref, prev_ref = refs[k:k + 7]
    j = pl.program_id(1)
    c = CHUNK
    mm = functools.partial(_mm, precise=precise)

    @pl.when(j == 0)
    def _():
        if has_past:
            st_ref[0] = st0_ref[0]
            ct_ref[0] = ct0_ref[0]
            ns_ref[0] = ns0_ref[0]
            ms_ref[0] = ms0_ref[0]
            prev_ref[...] = cv0_ref[0]
        else:
            st_ref[...] = jnp.zeros(st_ref.shape, F32)
            ct_ref[...] = jnp.zeros(ct_ref.shape, F32)
            ns_ref[...] = jnp.zeros(ns_ref.shape, F32)
            ms_ref[...] = jnp.zeros(ms_ref.shape, F32)
            prev_ref[...] = jnp.zeros(prev_ref.shape, F32)

    def load(ref):
        x = ref[...]
        if rows < c:
            x = jnp.concatenate([x, jnp.zeros((c - rows, x.shape[1]), F32)], axis=0)
        return x

    row_i = lax.broadcasted_iota(jnp.int32, (c, 1), 0)
    grow = j * c + row_i
    valid = (grow >= lo_row) & (grow < hi_row)
    lane128 = lax.broadcasted_iota(jnp.int32, (1, LANE), 1)
    lane256 = lax.broadcasted_iota(jnp.int32, (1, 256), 1)
    t_i = lax.broadcasted_iota(jnp.int32, (c, c), 0)
    s_i = lax.broadcasted_iota(jnp.int32, (c, c), 1)
    causal = s_i <= t_i
    mats = mats_ref[...]
    tri = mats[:c]
    misc = load(misc_ref)

    la = jax.nn.log_sigmoid(mm(misc, wa_ref[...], NN_DIMS) + ba_ref[...]) * (1.0 / GLA_TAU)
    la = jnp.where(valid, la, 0.0)
    gq = load(gq_ref) * (GLA_DK ** -0.5)
    gk = jnp.where(valid, load(gk_ref), 0.0)
    gv = jnp.where(valid, load(gv_ref), 0.0)
    ex = _dot_exact_lhs(mats, la)
    b = ex[:c]
    ghead = [(lane128 >= GLA_DK * h) & (lane128 < GLA_DK * (h + 1)) for h in range(GLA_HEADS)]
    vhead = [(lane256 >= GLA_DV * h) & (lane256 < GLA_DV * (h + 1)) for h in range(GLA_HEADS)]
    tmod = lax.broadcasted_iota(jnp.int32, (GLA_HEADS * c, c), 0) % c
    smod = lax.broadcasted_iota(jnp.int32, (GLA_HEADS * c, c), 1)

    def expand(x):
        return jnp.concatenate([jnp.where(ghead[h], x, 0.0) for h in range(GLA_HEADS)], axis=0)

    a_all = jnp.where(tmod == smod, mm(expand(gq), gk, NT_DIMS), 0.0)
    for lv in range(N_LEVELS):
        if (c >> (lv + 1)) >= rows:
            continue
        e = jnp.exp(ex[(lv + 1) * c:(lv + 2) * c])
        bit = N_LEVELS - 1 - lv
        second = ((row_i >> bit) & 1) == 1
        qt = jnp.where(second, gq * e, 0.0)
        kt = jnp.where(second, 0.0, gk * e)
        a_lv = mm(expand(qt), kt, NT_DIMS)
        a_all = a_all + jnp.where((tmod >> (bit + 1)) == (smod >> (bit + 1)), a_lv, 0.0)
    o = jnp.zeros((c, 256), F32)
    for h in range(GLA_HEADS):
        o = o + jnp.where(vhead[h], mm(a_all[h * c:(h + 1) * c], gv, NN_DIMS), 0.0)
    st = st_ref[0]
    o = o + mm(gq * jnp.exp(b), st, NT_DIMS)
    bl = b[c - 1:c]
    kl = gk * jnp.exp(bl - b)
    bd_gla = ((lax.broadcasted_iota(jnp.int32, (256, LANE), 0) // GLA_DV)
              == (lax.broadcasted_iota(jnp.int32, (256, LANE), 1) // GLA_DK))
    st_ref[0] = st * jnp.exp(bl) + jnp.where(bd_gla, mm(gv, kl, TN_DIMS), 0.0)
    ms2 = jnp.zeros((c, 256), F32)
    for h in range(GLA_HEADS):
        s2 = jnp.sum(jnp.where(vhead[h], o * o, 0.0), axis=-1, keepdims=True) * (1.0 / GLA_DV)
        ms2 = ms2 + jnp.where(vhead[h], lax.rsqrt(s2 + RMS_EPS), 0.0)
    ogla = o * ms2 * gnorm_ref[...] * jax.nn.silu(load(gr_ref))
    ogla_ref[...] = ogla[:rows]

    mu = jnp.where(valid, load(mu_ref), 0.0)
    up = jnp.concatenate([prev_ref[...], mu], axis=0)
    cw = cw_ref[...]
    uc = cb_ref[...]
    for i in range(CONV_W):
        off = 8 - (CONV_W - 1) + i
        uc = uc + up[off:off + c] * cw[i:i + 1]
    uc = jax.nn.silu(uc)
    if rows == c:
        prev_ref[...] = mu[c - 8:]
    mq = mm(uc, wq_ref[...], NN_DIMS)
    mk = mm(uc, wk_ref[...], NN_DIMS) * (ML_DH ** -0.5)
    mv = load(mv_ref)
    g_raw = misc + gbias_ref[...]
    is_i = (lane128 >= MS_MI) & (lane128 < MS_MI + ML_HEADS)
    is_f = (lane128 >= MS_MF) & (lane128 < MS_MF + ML_HEADS)
    g = jnp.where(is_i, jnp.where(valid, g_raw, NEG_BIG),
                  jnp.where(is_f, jnp.where(valid, jax.nn.log_sigmoid(g_raw), 0.0), 0.0))
    fc_all = _dot_exact_lhs(tri, jnp.where(is_f, g, 0.0))
    gt = g.T
    ft = fc_all.T
    ms_old = ms_ref[0]
    ns_old = ns_ref[0][0:1]
    ct = ct_ref[0]
    inter = mm(mq, ct, NT_DIMS)
    qn = mq * ns_old
    hc = jnp.zeros((c, 256), F32)
    ws_full = jnp.zeros((c, 256), F32)
    wl_row = jnp.zeros((1, 256), F32)
    ms_new = jnp.zeros((1, LANE), F32)
    for h in range(ML_HEADS):
        fcol = fc_all[:, MS_MF + h:MS_MF + h + 1]
        frow = ft[MS_MF + h:MS_MF + h + 1]
        igrow = gt[MS_MI + h:MS_MI + h + 1]
        igcol = g[:, MS_MI + h:MS_MI + h + 1]
        m0 = ms_old[0:1, h:h + 1]
        d = jnp.where(causal, fcol - frow + igrow, NEG_BIG)
        a = fcol + m0
        m = jnp.maximum(a, jnp.max(d, axis=-1, keepdims=True))
        wp = jnp.exp(a - m)
        w = jnp.exp(d - m)
        qk = mm(jnp.where(vhead[h], mq, 0.0), mk, NT_DIMS) * w
        num = wp * inter + mm(qk, mv, NN_DIMS)
        den = wp * jnp.sum(jnp.where(vhead[h], qn, 0.0), axis=-1, keepdims=True) + jnp.sum(qk, axis=-1, keepdims=True)
        hh = num / jnp.maximum(jnp.abs(den), jnp.exp(-m))
        hc = hc + jnp.where(vhead[h], hh, 0.0)
        m_last = m[c - 1:c]
        f_last = fcol[c - 1:c]
        wl = jnp.exp(f_last + m0 - m_last)
        ws = jnp.exp(f_last - fcol + igcol - m_last)
        ws_full = ws_full + jnp.where(vhead[h], ws, 0.0)
        wl_row = wl_row + jnp.where(vhead[h], wl, 0.0)
        ms_new = ms_new + jnp.where(lane128 == h, m_last, 0.0)
    kw = mk * ws_full
    bd_ml = ((lax.broadcasted_iota(jnp.int32, (256, 256), 0) // ML_DH)
             == (lax.broadcasted_iota(jnp.int32, (256, 256), 1) // ML_DH))
    ct_ref[0] = ct * wl_row + jnp.where(bd_ml, mm(mv, kw, TN_DIMS), 0.0)
    ns_new = wl_row * ns_old + jnp.sum(kw, axis=0, keepdims=True)
    ns_ref[0] = jnp.broadcast_to(ns_new, (8, 256))
    ms_ref[0] = jnp.broadcast_to(ms_new, (8, LANE))
    hc = hc * jax.nn.sigmoid(load(mo_ref))
    hn = jnp.zeros((c, 256), F32)
    for h in range(ML_HEADS):
        mean = jnp.sum(jnp.where(vhead[h], hc, 0.0), axis=-1, keepdims=True) * (1.0 / ML_DH)
        xc = jnp.where(vhead[h], hc - mean, 0.0)
        var = jnp.sum(xc * xc, axis=-1, keepdims=True) * (1.0 / ML_DH)
        hn = hn + xc * lax.rsqrt(var + LN_EPS)
    oml = hn * mnorm_ref[...] + mskip_ref[...] * uc
    oml_ref[...] = oml[:rows]


def _rec(proj, lw, consts, n_seq, n_blocks, rows, lo_row, hi_row, past, precise):
    n = proj.shape[0]
    has_past = past is not None
    rmap = lambda cidx: (lambda s, j: (s * n_blocks + j, cidx))
    in_specs = [pl.BlockSpec((rows, LANE), rmap(PJ_GQ)), pl.BlockSpec((rows, LANE), rmap(PJ_GK)),
                pl.BlockSpec((rows, 256), rmap(PJ_GV)), pl.BlockSpec((rows, 256), rmap(PJ_GR)),
                pl.BlockSpec((rows, 256), rmap(PJ_MU)), pl.BlockSpec((rows, 256), rmap(PJ_MV)),
                pl.BlockSpec((rows, 256), rmap(PJ_MO)), pl.BlockSpec((rows, LANE), rmap(PJ_MISC))]
    args = [proj] * 8
    weights = [consts["mats"], lw["wa"], lw["ba"], lw["gnorm"], lw["conv_w"], lw["conv_b"], lw["wq_bd"], lw["wk_bd"],
               lw["gbias"], lw["mnorm"], lw["mskip"]]
    in_specs += [_const_spec(w.shape) for w in weights]
    args += weights
    smap = lambda s, j: (s, 0, 0)
    state_shapes = [(n_seq, 256, LANE), (n_seq, 256, 256), (n_seq, 8, 256), (n_seq, 8, LANE)]
    if has_past:
        in_specs += [pl.BlockSpec((1,) + sh[1:], smap) for sh in state_shapes] + [pl.BlockSpec((1, 8, 256), smap)]
        args += list(past)
    out_specs = [pl.BlockSpec((rows, 256), rmap(0)), pl.BlockSpec((rows, 256), rmap(0))]
    out_specs += [pl.BlockSpec((1,) + sh[1:], smap) for sh in state_shapes]
    out_shape = [jax.ShapeDtypeStruct((n, 256), F32), jax.ShapeDtypeStruct((n, 256), F32)]
    out_shape += [jax.ShapeDtypeStruct(sh, F32) for sh in state_shapes]
    return pl.pallas_call(
        functools.partial(_rec_kernel, rows=rows, lo_row=lo_row, hi_row=hi_row, has_past=has_past, precise=precise),
        grid=(n_seq, n_blocks),
        in_specs=in_specs,
        out_specs=out_specs,
        out_shape=out_shape,
        scratch_shapes=[pltpu.VMEM((8, 256), F32)],
        compiler_params=_params(("parallel", "arbitrary")),
        name="rec",
    )(*args)


def _post_kernel(x_ref, olat_ref, ogla_ref, oml_ref, wo_ref, wo2_ref, l1g_ref, l1b_ref, rw_ref, rb_ref,
                 wgu_ref, wd_ref, l2g_ref, l2b_ref, o_ref):
    tm = x_ref.shape[0]
    gm = jnp.concatenate([ogla_ref[...], oml_ref[...]], axis=1)
    gb = gm.astype(BF16)
    if wo2_ref.shape[0] == 3 * gm.shape[1]:
        gb = jnp.concatenate([gb, (gm - gb.astype(F32)).astype(BF16), gb], axis=1)
    y = _dot(olat_ref[...], wo_ref[...]) + _dot(gb, wo2_ref[...])
    x1 = _layer_norm(ALPHA * x_ref[...] + y, l1g_ref[...], l1b_ref[...])
    xb = x1.astype(BF16)

    x_mid = (x1 - xb.astype(F32)).astype(BF16)
    lg = _dot(jnp.concatenate([xb, x_mid, xb], axis=1), rw_ref[...])
    sc = jax.nn.sigmoid(lg).T[:N_EXPERTS]
    sel = sc + rb_ref[...]
    ng, ne = N_GROUPS, E_PER_GROUP
    a_sel = [sel[ng * i:ng * (i + 1)] for i in range(ne)]
    top2 = None
    for i in range(ne):
        for k in range(i + 1, ne):
            ps = a_sel[i] + a_sel[k]
            top2 = ps if top2 is None else jnp.maximum(top2, ps)
    tg = [top2[g:g + 1] for g in range(ng)]
    best = functools.reduce(jnp.maximum, tg)
    gsel, taken = [], jnp.zeros_like(best)
    for g in range(ng):
        hit = jnp.where(tg[g] >= best, 1.0, 0.0) * (1.0 - taken)
        gsel.append(hit)
        taken = taken + hit
    cs = [sum(gsel[g] * jnp.where(gsel[g] > 0.0, sel[ng * i + g:ng * i + g + 1], 0.0) for g in range(ng))
          for i in range(ne)]
    ss = [sum(gsel[g] * jnp.where(gsel[g] > 0.0, sc[ng * i + g:ng * i + g + 1], 0.0) for g in range(ng))
          for i in range(ne)]
    gates = []
    for i in range(ne):
        rank = jnp.zeros_like(best)
        for k in range(ne):
            if k < i:
                rank = rank + jnp.where(cs[k] >= cs[i], 1.0, 0.0)
            elif k > i:
                rank = rank + jnp.where(cs[k] > cs[i], 1.0, 0.0)
        gates.append(jnp.where(rank < 1.5, ss[i], 0.0))
    inv = 1.0 / sum(gates)
    rows = [gates[i] * inv * gsel[g] for i in range(ne) for g in range(ng)]
    wt = jnp.concatenate(rows + [jnp.zeros((LANE - N_EXPERTS, tm), F32)], axis=0)
    wcol = wt.T

    acc = jnp.zeros((tm, x_ref.shape[1]), F32)
    for e in range(N_EXPERTS):
        col = ng * (e % ne) + e // ne
        gu = _dot(xb, wgu_ref[e])
        hh = jax.nn.silu(gu[:, :D_EXPERT]) * gu[:, D_EXPERT:] * wcol[:, col:col + 1]
        acc = acc + _dot(hh.astype(BF16), wd_ref[e])
    o_ref[...] = _layer_norm(ALPHA * x1 + acc, l2g_ref[...], l2b_ref[...])


def _post(x, olat, ogla, oml, lw, rw, rb, tm, precise):
    n, d = x.shape
    row = lambda i: (i, 0)
    single = dict(pipeline_mode=pl.Buffered(1))
    consts = [lw["wo"], lw["wo2_3"] if precise else lw["wo2"], lw["ln1_g"], lw["ln1_b"], rw, rb, lw["wgu"], lw["wd"], lw["ln2_g"], lw["ln2_b"]]
    in_specs = [pl.BlockSpec((tm, d), row), pl.BlockSpec((tm, 1024), row), pl.BlockSpec((tm, 256), row),
                pl.BlockSpec((tm, 256), row)]
    for w in consts:
        nd = w.ndim
        in_specs.append(pl.BlockSpec(w.shape, (lambda i, _nd=nd: (0,) * _nd), **single))
    return pl.pallas_call(
        _post_kernel,
        grid=(n // tm,),
        in_specs=in_specs,
        out_specs=pl.BlockSpec((tm, d), row),
        out_shape=jax.ShapeDtypeStruct((n, d), F32),
        compiler_params=_params(("parallel",)),
        name="post",
    )(x, olat, ogla, oml, *consts)


def _rope_tables(pos):
    half = MLA_ROPE // 2
    inv = ROPE_THETA ** (-jnp.arange(half, dtype=F32) / half)
    ang = pos.astype(F32)[:, None] * inv
    cos, sin = jnp.cos(ang), jnp.sin(ang)
    c2 = jnp.concatenate([cos, cos], axis=1)
    s2 = jnp.concatenate([-sin, sin], axis=1)
    cq = jnp.tile(c2, (1, MLA_HEADS))
    sq = jnp.tile(s2, (1, MLA_HEADS))
    kt = jnp.concatenate([c2, s2, jnp.zeros((pos.shape[0], LANE - 2 * MLA_ROPE), F32)], axis=1)
    return cq, sq, kt


def _prep_layer(l, p):
    d = p["w_in"].shape[1]
    offs = np.concatenate([[0], np.cumsum(IN_SIZES)])
    wi = p["w_in"][l]
    cq, ckv, kr, gq, gk, gv, ga, gr, mu, mv, mi, mf, mo = [wi[:, offs[i]:offs[i + 1]] for i in range(13)]
    half = MLA_ROPE // 2
    kr_sw = jnp.concatenate([kr[:, half:], kr[:, :half]], axis=1)
    misc = jnp.concatenate([kr, kr_sw, ga, mi, mf, jnp.zeros((d, LANE - MS_MF - ML_HEADS), F32)], axis=1)
    w_in32 = jnp.concatenate([cq, gv, gr, mu, mv, mo, ckv, gq, gk, misc], axis=1)
    w_in = w_in32.astype(BF16)
    w_in3 = _split_weight(w_in32) if l == 0 else None

    dq = MLA_NOPE + MLA_ROPE
    wuq = p["mla_w_uq"][l].reshape(MLA_Q_RANK, MLA_HEADS, dq)
    wukv = p["mla_w_ukv"][l].reshape(MLA_KV_RANK, MLA_HEADS, MLA_NOPE + MLA_V)
    wuk_t = jnp.transpose(wukv[..., :MLA_NOPE], (1, 2, 0))
    wql = _fold(jnp.transpose(wuq[..., :MLA_NOPE], (1, 0, 2)), wuk_t)
    wql = jnp.transpose(wql, (1, 0, 2)).reshape(MLA_Q_RANK, MLA_HEADS * MLA_KV_RANK)
    wqr = wuq[..., MLA_NOPE:]
    wqr_sw = jnp.concatenate([wqr[..., half:], wqr[..., :half]], axis=-1)
    wq = jnp.concatenate([wql, wqr.reshape(MLA_Q_RANK, -1), wqr_sw.reshape(MLA_Q_RANK, -1)], axis=1).astype(BF16)

    wout = p["w_out"][l]
    nv = MLA_HEADS * MLA_V
    wuv = jnp.transpose(wukv[..., MLA_NOPE:], (1, 0, 2))
    wo_mla = _fold(wuv, wout[:nv].reshape(MLA_HEADS, MLA_V, d)).reshape(MLA_HEADS * MLA_KV_RANK, d)
    wo = wo_mla.astype(BF16)
    wo2 = wout[nv:].astype(BF16)
    wo2_3 = _split_weight(wout[nv:]) if l == 0 else None

    wa = jnp.zeros((LANE, LANE), F32).at[MS_GA:MS_GA + GLA_GATE_RANK].set(p["gla_w_a"][l])
    gbias = (jnp.zeros((1, LANE), F32).at[0, MS_MI:MS_MI + ML_HEADS].set(p["ml_b_i"][l])
             .at[0, MS_MF:MS_MF + ML_HEADS].set(p["ml_b_f"][l]))

    def block_diag(w):
        out = jnp.zeros((ML_WIDTH, ML_WIDTH), F32)
        for h in range(ML_HEADS):
            out = out.at[h * ML_DH:(h + 1) * ML_DH, h * ML_DH:(h + 1) * ML_DH].set(w[h])
        return out

    conv_w = jnp.concatenate([p["ml_conv_w"][l], jnp.zeros((8 - CONV_W, ML_WIDTH), F32)], axis=0)
    wgu = jnp.concatenate([p["moe_w_gate"][l], p["moe_w_up"][l]], axis=2).astype(BF16)
    return dict(
        w_in=w_in, w_in3=w_in3, wq=wq, wo=wo, wo2=wo2, wo2_3=wo2_3, wa=wa, gbias=gbias,
        qn=p["mla_q_norm"][l][None], kvn=p["mla_kv_norm"][l][None],
        ba=p["gla_b_a"][l][None], gnorm=jnp.tile(p["gla_norm"][l], GLA_HEADS)[None],
        conv_w=conv_w, conv_b=p["ml_conv_b"][l][None], wq_bd=block_diag(p["ml_w_q"][l]),
        wk_bd=block_diag(p["ml_w_k"][l]), mnorm=p["ml_norm"][l][None], mskip=p["ml_skip"][l][None],
        ln1_g=p["ln1_g"][l][None], ln1_b=p["ln1_b"][l][None], ln2_g=p["ln2_g"][l][None], ln2_b=p["ln2_b"][l][None],
        wgu=wgu, wd=p["moe_w_down"][l].astype(BF16),
    )


def _gla_state_out(st):
    n = st.shape[0]
    s5 = st.reshape(n, GLA_HEADS, GLA_DV, GLA_HEADS, GLA_DK)
    return jnp.stack([jnp.transpose(s5[:, h, :, h, :], (0, 2, 1)) for h in range(GLA_HEADS)], axis=1)


def _gla_state_in(s):
    n = s.shape[0]
    out = jnp.zeros((n, GLA_HEADS, GLA_DV, GLA_HEADS, GLA_DK), F32)
    for h in range(GLA_HEADS):
        out = out.at[:, h, :, h, :].set(jnp.transpose(s[:, h], (0, 2, 1)))
    return out.reshape(n, GLA_HEADS * GLA_DV, GLA_HEADS * GLA_DK)


def _ml_state_out(ct):
    n = ct.shape[0]
    c5 = ct.reshape(n, ML_HEADS, ML_DH, ML_HEADS, ML_DH)
    return jnp.stack([jnp.transpose(c5[:, h, :, h, :], (0, 2, 1)) for h in range(ML_HEADS)], axis=1)


def _ml_state_in(cs):
    n = cs.shape[0]
    out = jnp.zeros((n, ML_HEADS, ML_DH, ML_HEADS, ML_DH), F32)
    for h in range(ML_HEADS):
        out = out.at[:, h, :, h, :].set(jnp.transpose(cs[:, h], (0, 2, 1)))
    return out.reshape(n, ML_WIDTH, ML_WIDTH)


def _pick_tile(n, cands):
    for t in cands:
        if n % t == 0:
            return t
    raise ValueError(f"no row tile for {n} rows")


def kernel(x_prompt, x_sample, cache_kv_latent, cache_k_rope, state_gla, state_mlstm_C, state_mlstm_n, state_mlstm_m, state_mlstm_conv, page_table, meta_tokens, ln_in_g, ln_in_b, w_in, mla_q_norm, mla_w_uq, mla_kv_norm, mla_w_ukv, gla_w_a, gla_b_a, gla_norm, ml_conv_w, ml_conv_b, ml_w_q, ml_w_k, ml_b_i, ml_b_f, ml_norm, ml_skip, w_out, ln1_g, ln1_b, ln2_g, ln2_b, router_w, router_bias, moe_w_gate, moe_w_up, moe_w_down):
    p = dict(w_in=w_in, mla_q_norm=mla_q_norm, mla_w_uq=mla_w_uq, mla_kv_norm=mla_kv_norm, mla_w_ukv=mla_w_ukv,
             gla_w_a=gla_w_a, gla_b_a=gla_b_a, gla_norm=gla_norm, ml_conv_w=ml_conv_w, ml_conv_b=ml_conv_b,
             ml_w_q=ml_w_q, ml_w_k=ml_w_k, ml_b_i=ml_b_i, ml_b_f=ml_b_f, ml_norm=ml_norm, ml_skip=ml_skip,
             w_out=w_out, ln1_g=ln1_g, ln1_b=ln1_b, ln2_g=ln2_g, ln2_b=ln2_b, moe_w_gate=moe_w_gate,
             moe_w_up=moe_w_up, moe_w_down=moe_w_down)
    depth = w_in.shape[0]
    nb, seq, d = x_prompt.shape
    ns, t_new, _ = x_sample.shape
    n_pages = page_table.shape[1]
    page = cache_kv_latent.shape[2]
    past_len = n_pages * page
    lp = PAD + seq
    assert seq % CHUNK == 0 and t_new == 8 and t_new >= CONV_W - 1

    layers = [_prep_layer(l, p) for l in range(depth)]
    consts = dict(mats=jnp.asarray(_level_mats(), BF16))
    perm = np.array([E_PER_GROUP * g + i for i in range(E_PER_GROUP) for g in range(N_GROUPS)])
    rw_hi, rw_mid = _split2_bits(router_w.astype(F32)[:, perm])
    rw = jnp.pad(jnp.concatenate([rw_hi, rw_hi, rw_mid], axis=0), ((0, 0), (0, LANE - N_EXPERTS)))
    rb = router_bias.astype(F32)[perm][:, None]
    g_in, b_in = ln_in_g[None], ln_in_b[None]

    meta = jnp.broadcast_to(meta_tokens[None].astype(F32), (nb, N_META, d))
    xp = jnp.concatenate([jnp.zeros((nb, PAD - N_META, d), F32), meta, x_prompt], axis=1).reshape(nb * lp, d)
    tabs_p = _rope_tables(jnp.arange(lp) - (PAD - N_META))
    tm_p = _pick_tile(nb * lp, (512, 384, 256, 128))
    tk_p = _pick_tile(lp, (384, 128))
    p_out = dict(kv=[], kr=[], gla=[], C=[], n=[], m=[], conv=[])
    for l in range(depth):
        lw = layers[l]
        precise = l == 0
        proj, xp = _in_proj(xp, g_in, b_in, lw["w_in3"] if precise else lw["w_in"], l == 0, tm_p)
        qlat, qrope, ckvn, krope, kcat, vt = _mla_prep(proj, tabs_p, lw["qn"], lw["kvn"], lw["wq"], tk_p, lp // tk_p)
        olat = _attn_prompt(qlat, qrope, kcat, vt, nb, lp, CHUNK, tk_p)
        ogla, oml, st, ct, nst, mst = _rec(proj, lw, consts, nb, lp // CHUNK, CHUNK, PAD - N_META, lp, None, precise)
        xp = _post(xp, olat, ogla, oml, lw, rw, rb, tm_p, precise)
        lo = PAD - N_META
        p_out["kv"].append(ckvn.reshape(nb, lp, -1)[:, lo:])
        p_out["kr"].append(krope.reshape(nb, lp, -1)[:, lo:])
        p_out["gla"].append(_gla_state_out(st))
        p_out["C"].append(_ml_state_out(ct))
        p_out["n"].append(nst[:, 0].reshape(nb, ML_HEADS, ML_DH))
        p_out["m"].append(mst[:, 0, :ML_HEADS])
        mu_cols = proj[:, PJ_MU * 256:(PJ_MU + 1) * 256].reshape(nb, lp, ML_WIDTH)
        p_out["conv"].append(mu_cols[:, lp - (CONV_W - 1):])
    y_prompt = xp.reshape(nb, lp, d)[:, PAD:]

    xs = x_sample.reshape(ns * t_new, d)
    tabs_s = _rope_tables(past_len + jnp.arange(t_new))
    tm_s = _pick_tile(ns * t_new, (512, 256, 128, 64, 32, 16))
    tabs_s = tuple(jnp.tile(t, (tm_s // t_new, 1)) for t in tabs_s)
    cache_rt = jnp.swapaxes(cache_k_rope, 2, 3)
    s_out = dict(kv=[], kr=[], gla=[], C=[], n=[], m=[], conv=[])
    for l in range(depth):
        lw = layers[l]
        precise = l == 0
        proj, xs = _in_proj(xs, g_in, b_in, lw["w_in3"] if precise else lw["w_in"], l == 0, tm_s)
        qlat, qrope, ckvn, krope, kcat, _ = _mla_prep(proj, tabs_s, lw["qn"], lw["kvn"], lw["wq"], tm_s, 1)
        olat = _attn_sample(qlat, qrope, kcat, cache_kv_latent, cache_rt, page_table, t_new, l)
        ns0 = jnp.broadcast_to(state_mlstm_n[l].reshape(ns, 1, ML_WIDTH), (ns, 8, ML_WIDTH))
        ms0 = jnp.broadcast_to(jnp.pad(state_mlstm_m[l], ((0, 0), (0, LANE - ML_HEADS)))[:, None], (ns, 8, LANE))
        cv0 = jnp.pad(state_mlstm_conv[l], ((0, 0), (8 - (CONV_W - 1), 0), (0, 0)))
        past = (_gla_state_in(state_gla[l]), _ml_state_in(state_mlstm_C[l]), ns0, ms0, cv0)
        ogla, oml, st, ct, nst, mst = _rec(proj, lw, consts, ns, 1, t_new, 0, t_new, past, precise)
        xs = _post(xs, olat, ogla, oml, lw, rw, rb, tm_s, precise)
        s_out["kv"].append(ckvn.reshape(ns, t_new, -1))
        s_out["kr"].append(krope.reshape(ns, t_new, -1))
        s_out["gla"].append(_gla_state_out(st))
        s_out["C"].append(_ml_state_out(ct))
        s_out["n"].append(nst[:, 0].reshape(ns, ML_HEADS, ML_DH))
        s_out["m"].append(mst[:, 0, :ML_HEADS])
        mu_cols = proj[:, PJ_MU * 256:(PJ_MU + 1) * 256].reshape(ns, t_new, ML_WIDTH)
        s_out["conv"].append(mu_cols[:, t_new - (CONV_W - 1):])
    y_sample = xs.reshape(ns, t_new, d)

    st_p = [jnp.stack(p_out[k]) for k in ("kv", "kr", "gla", "C", "n", "m", "conv")]
    st_s = [jnp.stack(s_out[k]) for k in ("kv", "kr", "gla", "C", "n", "m", "conv")]
    return (y_prompt, y_sample, *st_p, *st_s)
```

```python
import functools

import numpy as np
import jax
import jax.numpy as jnp
from jax import lax
from jax.experimental import pallas as pl
from jax.experimental.pallas import tpu as pltpu

F32 = jnp.float32
BF16 = jnp.bfloat16

N_META = 16
MLA_HEADS = 8
MLA_Q_RANK = 256
MLA_KV_RANK = 128
MLA_NOPE = 64
MLA_ROPE = 32
MLA_V = 64
MLA_SCALE = (MLA_NOPE + MLA_ROPE) ** -0.5
LOG2E = float(np.log2(np.e))
ROPE_THETA = 10000.0
GLA_HEADS = 4
GLA_DK = 32
GLA_DV = 64
GLA_GATE_RANK = 16
GLA_TAU = 16.0
ML_HEADS = 4
ML_DH = 64
ML_WIDTH = ML_HEADS * ML_DH
CONV_W = 4
N_EXPERTS = 16
N_GROUPS = 4
E_PER_GROUP = N_EXPERTS // N_GROUPS
D_EXPERT = 256
DEPTH = 2
ALPHA = (2 * DEPTH) ** 0.25
LN_EPS = 1e-5
RMS_EPS = 1e-6
IN_SIZES = (MLA_Q_RANK, MLA_KV_RANK, MLA_ROPE,
            GLA_HEADS * GLA_DK, GLA_HEADS * GLA_DK, GLA_HEADS * GLA_DV, GLA_GATE_RANK, GLA_HEADS * GLA_DV,
            ML_WIDTH, ML_WIDTH, ML_HEADS, ML_HEADS, ML_WIDTH)

LANE = 128
CHUNK = 128
PAD = 128
D_PROJ = 2048
PJ_CQ, PJ_GV, PJ_GR, PJ_MU, PJ_MV, PJ_MO = 0, 1, 2, 3, 4, 5
PJ_CKV, PJ_GQ, PJ_GK, PJ_MISC = 12, 13, 14, 15
MS_KR, MS_KRSW, MS_GA, MS_MI, MS_MF = 0, 32, 64, 80, 84
NEG_BIG = -1e30
VMEM_LIMIT = 56 * 1024 * 1024

NT_DIMS = (((1,), (1,)), ((), ()))
TN_DIMS = (((0,), (0,)), ((), ()))


def _dot(a, b):
    return jnp.dot(a, b, preferred_element_type=F32)


def _dot_nt(a, b):
    return lax.dot_general(a, b, NT_DIMS, preferred_element_type=F32)


def _dot_tn(a, b):
    return lax.dot_general(a, b, TN_DIMS, preferred_element_type=F32)


NN_DIMS = (((1,), (0,)), ((), ()))


def _mm(a, b, dims, precise):
    ah, bh = a.astype(BF16), b.astype(BF16)
    out = lax.dot_general(ah, bh, dims, preferred_element_type=F32)
    if precise:
        am = (a - ah.astype(F32)).astype(BF16)
        bm = (b - bh.astype(F32)).astype(BF16)
        out = (out + lax.dot_general(am, bh, dims, preferred_element_type=F32)
               + lax.dot_general(ah, bm, dims, preferred_element_type=F32))
    return out


def _split3(x):
    hi = x.astype(BF16)
    r = x - hi.astype(F32)
    mid = r.astype(BF16)
    lo = (r - mid.astype(F32)).astype(BF16)
    return hi, mid, lo


def _dot_exact_lhs(m01, x):
    hi, mid, lo = _split3(x)
    return _dot(m01, hi) + _dot(m01, mid) + _dot(m01, lo)


def _layer_norm(x, g, b):
    mu = jnp.mean(x, axis=-1, keepdims=True)
    xc = x - mu
    var = jnp.mean(xc * xc, axis=-1, keepdims=True)
    return xc * lax.rsqrt(var + LN_EPS) * g + b


def _params(sem):
    return pltpu.CompilerParams(dimension_semantics=sem, vmem_limit_bytes=VMEM_LIMIT)


def _const_spec(shape):
    nd = len(shape)
    return pl.BlockSpec(shape, lambda *_: (0,) * nd)


def _fold_kernel(a_ref, b_ref, o_ref):
    o_ref[0] = jnp.dot(a_ref[0], b_ref[0], preferred_element_type=F32, precision=lax.Precision.HIGHEST)


def _fold(a, b):
    h, m, k = a.shape
    n = b.shape[2]
    return pl.pallas_call(
        _fold_kernel,
        grid=(h,),
        in_specs=[pl.BlockSpec((1, m, k), lambda i: (i, 0, 0)), pl.BlockSpec((1, k, n), lambda i: (i, 0, 0))],
        out_specs=pl.BlockSpec((1, m, n), lambda i: (i, 0, 0)),
        out_shape=jax.ShapeDtypeStruct((h, m, n), F32),
        compiler_params=_params(("arbitrary",)),
        name="fold",
    )(a, b)


def _in_proj_kernel(x_ref, g_ref, b_ref, w_ref, proj_ref, *xn_refs, ln_in):
    x = x_ref[...]
    if ln_in:
        x = _layer_norm(x, g_ref[...], b_ref[...])
        xn_refs[0][...] = x
    xb = x.astype(BF16)
    if w_ref.shape[0] == 3 * x.shape[1]:
        x_mid = (x - xb.astype(F32)).astype(BF16)
        xb = jnp.concatenate([xb, x_mid, xb], axis=1)
    proj_ref[...] = _dot(xb, w_ref[...])


def _split2_bits(w):
    bits = lax.bitcast_convert_type(w.astype(F32), jnp.uint32) & jnp.uint32(0xFFFF0000)
    hi = lax.bitcast_convert_type(bits, F32)
    return hi.astype(BF16), (w - hi).astype(BF16)


def _split_weight(w):
    hi, mid = _split2_bits(w)
    return jnp.concatenate([hi, hi, mid], axis=0)


def _in_proj(x, g, b, w, ln_in, tm):
    n, d = x.shape
    out_shape = [jax.ShapeDtypeStruct((n, D_PROJ), F32)]
    out_specs = [pl.BlockSpec((tm, D_PROJ), lambda i: (i, 0))]
    if ln_in:
        out_shape.append(jax.ShapeDtypeStruct((n, d), F32))
        out_specs.append(pl.BlockSpec((tm, d), lambda i: (i, 0)))
    res = pl.pallas_call(
        functools.partial(_in_proj_kernel, ln_in=ln_in),
        grid=(n // tm,),
        in_specs=[pl.BlockSpec((tm, d), lambda i: (i, 0)), _const_spec((1, d)), _const_spec((1, d)),
                  pl.BlockSpec(w.shape, lambda i: (0, 0), pipeline_mode=pl.Buffered(1))],
        out_specs=out_specs,
        out_shape=out_shape,
        compiler_params=_params(("parallel",)),
        name="in_proj",
    )(x, g, b, w)
    return (res[0], res[1]) if ln_in else (res[0], x)


def _mla_prep_kernel(cq_ref, ckv_ref, misc_ref, cq_tab, sq_tab, k_tab, qn_ref, kvn_ref, wq_ref,
                     qlat_ref, qrope_ref, ckvn_ref, krope_ref, kcat_ref, vt_ref, *, qscale):
    cq = cq_ref[...]
    cqn = cq * lax.rsqrt(jnp.mean(cq * cq, axis=-1, keepdims=True) + RMS_EPS) * qn_ref[...]
    q = _dot(cqn.astype(BF16), wq_ref[...])
    nl = MLA_HEADS * MLA_KV_RANK
    nr = MLA_HEADS * MLA_ROPE
    qlat_ref[...] = (q[:, :nl] * qscale).astype(BF16)
    rot = q[:, nl:nl + nr] * cq_tab[...] + q[:, nl + nr:] * sq_tab[...]
    qrope_ref[...] = (rot * qscale).astype(BF16)
    ckv = ckv_ref[...]
    ckvn = ckv * lax.rsqrt(jnp.mean(ckv * ckv, axis=-1, keepdims=True) + RMS_EPS) * kvn_ref[...]
    ckvn_ref[...] = ckvn
    p = misc_ref[...] * k_tab[...]
    kr = p[:, MS_KR:MS_KR + MLA_ROPE] + p[:, MS_KRSW:MS_KRSW + MLA_ROPE]
    krope_ref[...] = kr
    kcat_ref[...] = jnp.concatenate([ckvn, kr], axis=1).astype(BF16)
    vt_ref[0] = ckvn.T.astype(BF16)


def _mla_prep(proj, tabs, qn, kvn, wq, tp, tab_blocks, qscale):
    n = proj.shape[0]
    cq_tab, sq_tab, k_tab = tabs
    nq = wq.shape[1]
    dk = MLA_KV_RANK + MLA_ROPE
    row = lambda c: (lambda i: (i, c))
    tab = lambda i: (i % tab_blocks, 0)
    return pl.pallas_call(
        functools.partial(_mla_prep_kernel, qscale=qscale),
        grid=(n // tp,),
        in_specs=[pl.BlockSpec((tp, 256), row(PJ_CQ)), pl.BlockSpec((tp, LANE), row(PJ_CKV)),
                  pl.BlockSpec((tp, LANE), row(PJ_MISC)),
                  pl.BlockSpec((tp, 256), tab), pl.BlockSpec((tp, 256), tab), pl.BlockSpec((tp, LANE), tab),
                  _const_spec((1, MLA_Q_RANK)), _const_spec((1, MLA_KV_RANK)), _const_spec((MLA_Q_RANK, nq))],
        out_specs=[pl.BlockSpec((tp, 1024), row(0)), pl.BlockSpec((tp, 256), row(0)),
                   pl.BlockSpec((tp, MLA_KV_RANK), row(0)), pl.BlockSpec((tp, MLA_ROPE), row(0)),
                   pl.BlockSpec((tp, dk), row(0)), pl.BlockSpec((1, MLA_KV_RANK, tp), lambda i: (i, 0, 0))],
        out_shape=[jax.ShapeDtypeStruct((n, 1024), BF16), jax.ShapeDtypeStruct((n, 256), BF16),
                   jax.ShapeDtypeStruct((n, MLA_KV_RANK), F32), jax.ShapeDtypeStruct((n, MLA_ROPE), F32),
                   jax.ShapeDtypeStruct((n, dk), BF16), jax.ShapeDtypeStruct((n // tp, MLA_KV_RANK, tp), BF16)],
        compiler_params=_params(("parallel",)),
        name="mla_prep",
    )(proj, proj, proj, cq_tab, sq_tab, k_tab, qn, kvn, wq)


def _attn_prompt_kernel(ql_ref, qr_ref, k_ref, vt_ref, o_ref, m_ref, l_ref, acc_ref, sa_ref, sb_ref, *, tq, tk,
                        lo_key):
    qi = pl.program_id(1)
    m_rows = tq * MLA_HEADS
    ql = ql_ref[...]
    qr = qr_ref[...]
    q = jnp.concatenate(
        [jnp.concatenate([ql[:, h * MLA_KV_RANK:(h + 1) * MLA_KV_RANK], qr[:, h * MLA_ROPE:(h + 1) * MLA_ROPE]], axis=1)
         for h in range(MLA_HEADS)], axis=0)
    qpos = qi * tq + lax.broadcasted_iota(jnp.int32, (1, m_rows), 1) % tq
    lowk = jnp.where(qpos < lo_key, 0, lo_key)

    def scores(j):
        start = j * tk if isinstance(j, int) else pl.multiple_of(j * tk, tk)
        return _dot_nt(k_ref[pl.ds(start, tk), :], q)

    def consume(s, j, masked, first):
        if masked:
            kpos = j * tk + lax.broadcasted_iota(jnp.int32, (tk, 1), 0)
            s = jnp.where(kpos <= qpos, jnp.where(kpos >= lowk, s, NEG_BIG), NEG_BIG)
        smax = jnp.max(s, axis=0, keepdims=True)
        if first:
            m_new = smax
            p = jnp.exp2(s - m_new)
            l_ref[...] = jnp.sum(p, axis=0, keepdims=True)
            acc_ref[...] = _dot(vt_ref[j], p.astype(BF16))
        else:
            m_old = m_ref[...]
            m_new = jnp.maximum(m_old, smax)
            a = jnp.exp2(m_old - m_new)
            p = jnp.exp2(s - m_new)
            l_ref[...] = l_ref[...] * a + jnp.sum(p, axis=0, keepdims=True)
            acc_ref[...] = acc_ref[...] * a + _dot(vt_ref[j], p.astype(BF16))
        m_ref[...] = m_new

    jd = (qi * tq) // tk
    consume(scores(0), 0, True, True)

    n_mid = jnp.maximum(jd - 1, 0)
    nk = vt_ref.shape[0]

    @pl.when(n_mid > 0)
    def _():
        sa_ref[...] = scores(jnp.minimum(1, nk - 1))

    def pair(i, c):
        t1 = 1 + 2 * i
        sb_ref[...] = scores(t1 + 1)
        consume(sa_ref[...], t1, False, False)
        sa_ref[...] = scores(jnp.minimum(t1 + 2, nk - 1))
        consume(sb_ref[...], t1 + 1, False, False)
        return c

    lax.fori_loop(0, n_mid // 2, pair, 0)

    @pl.when(n_mid % 2 == 1)
    def _():
        consume(sa_ref[...], jd - 1, False, False)

    @pl.when(jd > 0)
    def _():
        consume(scores(jd), jd, True, False)

    o = acc_ref[...] / l_ref[...]
    for h in range(MLA_HEADS):
        o_ref[:, h * MLA_KV_RANK:(h + 1) * MLA_KV_RANK] = o[:, h * tq:(h + 1) * tq].T.astype(BF16)


def _attn_prompt(qlat, qrope, kcat, vt, n_seq, lp, tq, tk):
    m_rows = tq * MLA_HEADS
    nq = lp // tq
    nk = lp // tk
    dk = MLA_KV_RANK + MLA_ROPE
    qmap = lambda b, i: (b * nq + i, 0)
    return pl.pallas_call(
        functools.partial(_attn_prompt_kernel, tq=tq, tk=tk, lo_key=PAD - N_META),
        grid=(n_seq, nq),
        in_specs=[pl.BlockSpec((tq, MLA_HEADS * MLA_KV_RANK), qmap), pl.BlockSpec((tq, MLA_HEADS * MLA_ROPE), qmap),
                  pl.BlockSpec((lp, dk), lambda b, i: (b, 0)),
                  pl.BlockSpec((nk, MLA_KV_RANK, tk), lambda b, i: (b, 0, 0))],
        out_specs=pl.BlockSpec((tq, MLA_HEADS * MLA_KV_RANK), qmap),
        out_shape=jax.ShapeDtypeStruct(qlat.shape, BF16),
        scratch_shapes=[pltpu.VMEM((1, m_rows), F32), pltpu.VMEM((1, m_rows), F32),
                        pltpu.VMEM((MLA_KV_RANK, m_rows), F32),
                        pltpu.VMEM((tk, m_rows), F32), pltpu.VMEM((tk, m_rows), F32)],
        compiler_params=_params(("parallel", "arbitrary")),
        name="attn_prompt",
    )(qlat, qrope, kcat, vt)


def _attn_sample_kernel(pt_ref, ql_ref, qr_ref, knew_ref, cache_c, cache_r, o_ref,
                        bufc, bufr, semc, semr, m_ref, l_ref, acc_ref, *, n_pages, gp, t_new, page, layer):
    s_id = pl.program_id(0)
    n_seq = pl.num_programs(0)
    ng = n_pages // gp
    gk = gp * page

    def copies(seq, g, slot):
        out = []
        for p in range(gp):
            pg = pt_ref[seq * n_pages + g * gp + p]
            out.append(pltpu.make_async_copy(cache_c.at[layer, pg], bufc.at[slot, pl.ds(p * page, page)],
                                             semc.at[slot]))
            out.append(pltpu.make_async_copy(cache_r.at[layer, pg], bufr.at[slot, p], semr.at[slot]))
        return out

    @pl.when(s_id == 0)
    def _():
        for c in copies(0, 0, 0):
            c.start()

    ql = ql_ref[...]
    qr = qr_ref[...]
    q = jnp.concatenate([ql, qr], axis=1)
    m_ref[...] = jnp.full(m_ref.shape, NEG_BIG, F32)
    l_ref[...] = jnp.zeros(l_ref.shape, F32)
    acc_ref[...] = jnp.zeros(acc_ref.shape, F32)

    def update(s, v):
        m_old = m_ref[...]
        m_new = jnp.maximum(m_old, jnp.max(s, axis=-1, keepdims=True))
        a = jnp.exp(m_old - m_new)
        p = jnp.exp(s - m_new)
        l_ref[...] = l_ref[...] * a + jnp.sum(p, axis=-1, keepdims=True)
        acc_ref[...] = acc_ref[...] * a + _dot(p.astype(BF16), v)
        m_ref[...] = m_new

    def group(g, c):
        it = s_id * ng + g
        slot = it % 2
        nxt_seq = jnp.where(g + 1 < ng, s_id, s_id + 1)
        nxt_g = jnp.where(g + 1 < ng, g + 1, 0)

        @pl.when(nxt_seq < n_seq)
        def _():
            for cp in copies(nxt_seq, nxt_g, 1 - slot):
                cp.start()

        for cp in copies(s_id, g, slot):
            cp.wait()
        n_half = 2 if gp % 2 == 0 else 1
        hp = gp // n_half
        halves = []
        for i in range(n_half):
            kc = bufc[slot, pl.ds(i * hp * page, hp * page)].astype(BF16)
            krt = jnp.concatenate([bufr[slot, p] for p in range(i * hp, (i + 1) * hp)], axis=1).astype(BF16)
            halves.append((_dot_nt(ql, kc) + _dot(qr, krt), kc))
        for s, kc in halves:
            update(s, kc)
        return c

    lax.fori_loop(0, ng, group, 0)

    kn = knew_ref[...]
    s = _dot_nt(q, kn)
    rows = q.shape[0]
    qt = lax.broadcasted_iota(jnp.int32, (rows, t_new), 0) // MLA_HEADS
    kt = lax.broadcasted_iota(jnp.int32, (rows, t_new), 1)
    s = jnp.where(kt <= qt, s, NEG_BIG)
    update(s, kn[:, :MLA_KV_RANK])
    o_ref[...] = (acc_ref[...] / l_ref[...]).astype(BF16)


def _attn_sample(qlat, qrope, kcat, cache_c, cache_r, page_table, t_new, layer):
    n_seq, n_pages = page_table.shape
    page = cache_c.shape[2]
    gp = _pick_tile(n_pages, (64, 32, 16, 8, 4, 2, 1))
    rows = t_new * MLA_HEADS
    dk = MLA_KV_RANK + MLA_ROPE
    ql = qlat.reshape(-1, MLA_KV_RANK)
    qr = qrope.reshape(-1, MLA_ROPE)
    grid_spec = pltpu.PrefetchScalarGridSpec(
        num_scalar_prefetch=1,
        grid=(n_seq,),
        in_specs=[pl.BlockSpec((rows, MLA_KV_RANK), lambda s, pt: (s, 0)),
                  pl.BlockSpec((rows, MLA_ROPE), lambda s, pt: (s, 0)),
                  pl.BlockSpec((t_new, dk), lambda s, pt: (s, 0)),
                  pl.BlockSpec(memory_space=pl.ANY), pl.BlockSpec(memory_space=pl.ANY)],
        out_specs=pl.BlockSpec((rows, MLA_KV_RANK), lambda s, pt: (s, 0)),
        scratch_shapes=[pltpu.VMEM((2, gp * page, MLA_KV_RANK), F32), pltpu.VMEM((2, gp, MLA_ROPE, page), F32),
                        pltpu.SemaphoreType.DMA((2,)), pltpu.SemaphoreType.DMA((2,)),
                        pltpu.VMEM((rows, 1), F32), pltpu.VMEM((rows, 1), F32),
                        pltpu.VMEM((rows, MLA_KV_RANK), F32)],
    )
    out = pl.pallas_call(
        functools.partial(_attn_sample_kernel, n_pages=n_pages, gp=gp, t_new=t_new, page=page, layer=layer),
        grid_spec=grid_spec,
        out_shape=jax.ShapeDtypeStruct(ql.shape, BF16),
        compiler_params=_params(("arbitrary",)),
        name="attn_sample",
    )(page_table.reshape(-1), ql, qr, kcat, cache_c, cache_r)
    return out.reshape(-1, MLA_HEADS * MLA_KV_RANK)


def _level_mats():
    c = CHUNK
    t = np.arange(c)[:, None]
    s = np.arange(c)[None, :]
    mats = [(s <= t)]
    h = c // 2
    while h >= 1:
        start = (t // (2 * h)) * (2 * h) + h
        second = (t % (2 * h)) >= h
        mats.append(np.where(second, (s >= start) & (s <= t), (s > t) & (s < start)))
        h //= 2
    return np.concatenate(mats, axis=0).astype(np.float32)


N_LEVELS = int(np.log2(CHUNK))


def _rec_kernel(*refs, rows, lo_row, hi_row, has_past, precise):
    (gq_ref, gk_ref, gv_ref, gr_ref, mu_ref, mv_ref, mo_ref, misc_ref,
     mats_ref, wa_ref, ba_ref, gnorm_ref, cw_ref, cb_ref, wq_ref, wk_ref, gbias_ref, mnorm_ref, mskip_ref) = refs[:19]
    k = 19
    if has_past:
        st0_ref, ct0_ref, ns0_ref, ms0_ref, cv0_ref = refs[k:k + 5]
        k += 5
    ogla_ref, oml_ref, st_ref, ct_ref, ns_ref, ms_ref, prev_ref = refs[k:k + 7]
    j = pl.program_id(1)
    c = CHUNK
    mm = functools.partial(_mm, precise=precise)

    @pl.when(j == 0)
    def _():
        if has_past:
            st_ref[0] = st0_ref[0]
            ct_ref[0] = ct0_ref[0]
            ns_ref[0] = ns0_ref[0]
            ms_ref[0] = ms0_ref[0]
            prev_ref[...] = cv0_ref[0]
        else:
            st_ref[...] = jnp.zeros(st_ref.shape, F32)
            ct_ref[...] = jnp.zeros(ct_ref.shape, F32)
            ns_ref[...] = jnp.zeros(ns_ref.shape, F32)
            ms_ref[...] = jnp.zeros(ms_ref.shape, F32)
            prev_ref[...] = jnp.zeros(prev_ref.shape, F32)

    def load(ref):
        x = ref[...]
        if rows < c:
            x = jnp.concatenate([x, jnp.zeros((c - rows, x.shape[1]), F32)], axis=0)
        return x

    row_i = lax.broadcasted_iota(jnp.int32, (c, 1), 0)
    grow = j * c + row_i
    valid = (grow >= lo_row) & (grow < hi_row)
    lane128 = lax.broadcasted_iota(jnp.int32, (1, LANE), 1)
    lane256 = lax.broadcasted_iota(jnp.int32, (1, 256), 1)
    t_i = lax.broadcasted_iota(jnp.int32, (c, c), 0)
    s_i = lax.broadcasted_iota(jnp.int32, (c, c), 1)
    causal = s_i <= t_i
    mats = mats_ref[...]
    tri = mats[:c]
    misc = load(misc_ref)

    la = jax.nn.log_sigmoid(mm(misc, wa_ref[...], NN_DIMS) + ba_ref[...]) * (1.0 / GLA_TAU)
    la = jnp.where(valid, la, 0.0)
    gq = load(gq_ref) * (GLA_DK ** -0.5)
    gk = jnp.where(valid, load(gk_ref), 0.0)
    gv = jnp.where(valid, load(gv_ref), 0.0)
    ex = _dot_exact_lhs(mats, la)
    b = ex[:c]
    ghead = [(lane128 >= GLA_DK * h) & (lane128 < GLA_DK * (h + 1)) for h in range(GLA_HEADS)]
    vhead = [(lane256 >= GLA_DV * h) & (lane256 < GLA_DV * (h + 1)) for h in range(GLA_HEADS)]
    tmod = lax.broadcasted_iota(jnp.int32, (GLA_HEADS * c, c), 0) % c
    smod = lax.broadcasted_iota(jnp.int32, (GLA_HEADS * c, c), 1)

    def expand(x):
        return jnp.concatenate([jnp.where(ghead[h], x, 0.0) for h in range(GLA_HEADS)], axis=0)

    a_all = jnp.where(tmod == smod, mm(expand(gq), gk, NT_DIMS), 0.0)
    for lv in range(N_LEVELS):
        if (c >> (lv + 1)) >= rows:
            continue
        e = jnp.exp(ex[(lv + 1) * c:(lv + 2) * c])
        bit = N_LEVELS - 1 - lv
        second = ((row_i >> bit) & 1) == 1
        qt = jnp.where(second, gq * e, 0.0)
        kt = jnp.where(second, 0.0, gk * e)
        a_lv = mm(expand(qt), kt, NT_DIMS)
        a_all = a_all + jnp.where((tmod >> (bit + 1)) == (smod >> (bit + 1)), a_lv, 0.0)
    o = jnp.zeros((c, 256), F32)
    for h in range(GLA_HEADS):
        o = o + jnp.where(vhead[h], mm(a_all[h * c:(h + 1) * c], gv, NN_DIMS), 0.0)
    st = st_ref[0]
    o = o + mm(gq * jnp.exp(b), st, NT_DIMS)
    bl = b[c - 1:c]
    kl = gk * jnp.exp(bl - b)
    bd_gla = ((lax.broadcasted_iota(jnp.int32, (256, LANE), 0) // GLA_DV)
              == (lax.broadcasted_iota(jnp.int32, (256, LANE), 1) // GLA_DK))
    st_ref[0] = st * jnp.exp(bl) + jnp.where(bd_gla, mm(gv, kl, TN_DIMS), 0.0)
    ms2 = jnp.zeros((c, 256), F32)
    for h in range(GLA_HEADS):
        s2 = jnp.sum(jnp.where(vhead[h], o * o, 0.0), axis=-1, keepdims=True) * (1.0 / GLA_DV)
        ms2 = ms2 + jnp.where(vhead[h], lax.rsqrt(s2 + RMS_EPS), 0.0)
    ogla = o * ms2 * gnorm_ref[...] * jax.nn.silu(load(gr_ref))
    ogla_ref[...] = ogla[:rows]

    mu = jnp.where(valid, load(mu_ref), 0.0)
    up = jnp.concatenate([prev_ref[...], mu], axis=0)
    cw = cw_ref[...]
    uc = cb_ref[...]
    for i in range(CONV_W):
        off = 8 - (CONV_W - 1) + i
        uc = uc + up[off:off + c] * cw[i:i + 1]
    uc = jax.nn.silu(uc)
    if rows == c:
        prev_ref[...] = mu[c - 8:]
    mq = mm(uc, wq_ref[...], NN_DIMS)
    mk = mm(uc, wk_ref[...], NN_DIMS) * (ML_DH ** -0.5)
    mv = load(mv_ref)
    g_raw = misc + gbias_ref[...]
    is_i = (lane128 >= MS_MI) & (lane128 < MS_MI + ML_HEADS)
    is_f = (lane128 >= MS_MF) & (lane128 < MS_MF + ML_HEADS)
    g = jnp.where(is_i, jnp.where(valid, g_raw, NEG_BIG),
                  jnp.where(is_f, jnp.where(valid, jax.nn.log_sigmoid(g_raw), 0.0), 0.0))
    fc_all = _dot_exact_lhs(tri, jnp.where(is_f, g, 0.0))
    gt = g.T
    ft = fc_all.T
    ms_old = ms_ref[0]
    ns_old = ns_ref[0][0:1]
    ct = ct_ref[0]
    inter = mm(mq, ct, NT_DIMS)
    qn = mq * ns_old
    hc = jnp.zeros((c, 256), F32)
    ws_full = jnp.zeros((c, 256), F32)
    wl_row = jnp.zeros((1, 256), F32)
    ms_new = jnp.zeros((1, LANE), F32)
    for h in range(ML_HEADS):
        fcol = fc_all[:, MS_MF + h:MS_MF + h + 1]
        frow = ft[MS_MF + h:MS_MF + h + 1]
        igrow = gt[MS_MI + h:MS_MI + h + 1]
        igcol = g[:, MS_MI + h:MS_MI + h + 1]
        m0 = ms_old[0:1, h:h + 1]
        d = jnp.where(causal, fcol - frow + igrow, NEG_BIG)
        a = fcol + m0
        m = jnp.maximum(a, jnp.max(d, axis=-1, keepdims=True))
        wp = jnp.exp(a - m)
        w = jnp.exp(d - m)
        qk = mm(jnp.where(vhead[h], mq, 0.0), mk, NT_DIMS) * w
        num = wp * inter + mm(qk, mv, NN_DIMS)
        den = wp * jnp.sum(jnp.where(vhead[h], qn, 0.0), axis=-1, keepdims=True) + jnp.sum(qk, axis=-1, keepdims=True)
        hh = num / jnp.maximum(jnp.abs(den), jnp.exp(-m))
        hc = hc + jnp.where(vhead[h], hh, 0.0)
        m_last = m[c - 1:c]
        f_last = fcol[c - 1:c]
        wl = jnp.exp(f_last + m0 - m_last)
        ws = jnp.exp(f_last - fcol + igcol - m_last)
        ws_full = ws_full + jnp.where(vhead[h], ws, 0.0)
        wl_row = wl_row + jnp.where(vhead[h], wl, 0.0)
        ms_new = ms_new + jnp.where(lane128 == h, m_last, 0.0)
    kw = mk * ws_full
    bd_ml = ((lax.broadcasted_iota(jnp.int32, (256, 256), 0) // ML_DH)
             == (lax.broadcasted_iota(jnp.int32, (256, 256), 1) // ML_DH))
    ct_ref[0] = ct * wl_row + jnp.where(bd_ml, mm(mv, kw, TN_DIMS), 0.0)
    ns_new = wl_row * ns_old + jnp.sum(kw, axis=0, keepdims=True)
    ns_ref[0] = jnp.broadcast_to(ns_new, (8, 256))
    ms_ref[0] = jnp.broadcast_to(ms_new, (8, LANE))
    hc = hc * jax.nn.sigmoid(load(mo_ref))
    hn = jnp.zeros((c, 256), F32)
    for h in range(ML_HEADS):
        mean = jnp.sum(jnp.where(vhead[h], hc, 0.0), axis=-1, keepdims=True) * (1.0 / ML_DH)
        xc = jnp.where(vhead[h], hc - mean, 0.0)
        var = jnp.sum(xc * xc, axis=-1, keepdims=True) * (1.0 / ML_DH)
        hn = hn + xc * lax.rsqrt(var + LN_EPS)
    oml = hn * mnorm_ref[...] + mskip_ref[...] * uc
    oml_ref[...] = oml[:rows]


def _rec(proj, lw, consts, n_seq, n_blocks, rows, lo_row, hi_row, past, precise):
    n = proj.shape[0]
    has_past = past is not None
    rmap = lambda cidx: (lambda s, j: (s * n_blocks + j, cidx))
    in_specs = [pl.BlockSpec((rows, LANE), rmap(PJ_GQ)), pl.BlockSpec((rows, LANE), rmap(PJ_GK)),
                pl.BlockSpec((rows, 256), rmap(PJ_GV)), pl.BlockSpec((rows, 256), rmap(PJ_GR)),
                pl.BlockSpec((rows, 256), rmap(PJ_MU)), pl.BlockSpec((rows, 256), rmap(PJ_MV)),
                pl.BlockSpec((rows, 256), rmap(PJ_MO)), pl.BlockSpec((rows, LANE), rmap(PJ_MISC))]
    args = [proj] * 8
    weights = [consts["mats"], lw["wa"], lw["ba"], lw["gnorm"], lw["conv_w"], lw["conv_b"], lw["wq_bd"], lw["wk_bd"],
               lw["gbias"], lw["mnorm"], lw["mskip"]]
    in_specs += [_const_spec(w.shape) for w in weights]
    args += weights
    smap = lambda s, j: (s, 0, 0)
    state_shapes = [(n_seq, 256, LANE), (n_seq, 256, 256), (n_seq, 8, 256), (n_seq, 8, LANE)]
    if has_past:
        in_specs += [pl.BlockSpec((1,) + sh[1:], smap) for sh in state_shapes] + [pl.BlockSpec((1, 8, 256), smap)]
        args += list(past)
    out_specs = [pl.BlockSpec((rows, 256), rmap(0)), pl.BlockSpec((rows, 256), rmap(0))]
    out_specs += [pl.BlockSpec((1,) + sh[1:], smap) for sh in state_shapes]
    out_shape = [jax.ShapeDtypeStruct((n, 256), F32), jax.ShapeDtypeStruct((n, 256), F32)]
    out_shape += [jax.ShapeDtypeStruct(sh, F32) for sh in state_shapes]
    return pl.pallas_call(
        functools.partial(_rec_kernel, rows=rows, lo_row=lo_row, hi_row=hi_row, has_past=has_past, precise=precise),
        grid=(n_seq, n_blocks),
        in_specs=in_specs,
        out_specs=out_specs,
        out_shape=out_shape,
        scratch_shapes=[pltpu.VMEM((8, 256), F32)],
        compiler_params=_params(("parallel", "arbitrary")),
        name="rec",
    )(*args)


def _post_kernel(x_ref, olat_ref, ogla_ref, oml_ref, wo_ref, wo2_ref, l1g_ref, l1b_ref, rw_ref, rb_ref,
                 wgu_ref, wd_ref, l2g_ref, l2b_ref, o_ref):
    tm = x_ref.shape[0]
    gm = jnp.concatenate([ogla_ref[...], oml_ref[...]], axis=1)
    gb = gm.astype(BF16)
    if wo2_ref.shape[0] == 3 * gm.shape[1]:
        gb = jnp.concatenate([gb, (gm - gb.astype(F32)).astype(BF16), gb], axis=1)
    y = _dot(olat_ref[...], wo_ref[...]) + _dot(gb, wo2_ref[...])
    x1 = _layer_norm(ALPHA * x_ref[...] + y, l1g_ref[...], l1b_ref[...])
    xb = x1.astype(BF16)

    xr = xb
    if rw_ref.shape[0] == 3 * xb.shape[1]:
        xr = jnp.concatenate([xb, (x1 - xb.astype(F32)).astype(BF16), xb], axis=1)
    lg = _dot(xr, rw_ref[...])
    sc = jax.nn.sigmoid(lg).T[:N_EXPERTS]
    sel = sc + rb_ref[...]
    ng, ne = N_GROUPS, E_PER_GROUP
    a_sel = [sel[ng * i:ng * (i + 1)] for i in range(ne)]
    top2 = None
    for i in range(ne):
        for k in range(i + 1, ne):
            ps = a_sel[i] + a_sel[k]
            top2 = ps if top2 is None else jnp.maximum(top2, ps)
    tg = [top2[g:g + 1] for g in range(ng)]
    best = functools.reduce(jnp.maximum, tg)
    gsel, taken = [], jnp.zeros_like(best)
    for g in range(ng):
        hit = jnp.where(tg[g] >= best, 1.0, 0.0) * (1.0 - taken)
        gsel.append(hit)
        taken = taken + hit
    cs = [sum(gsel[g] * jnp.where(gsel[g] > 0.0, sel[ng * i + g:ng * i + g + 1], 0.0) for g in range(ng))
          for i in range(ne)]
    ss = [sum(gsel[g] * jnp.where(gsel[g] > 0.0, sc[ng * i + g:ng * i + g + 1], 0.0) for g in range(ng))
          for i in range(ne)]
    gates = []
    for i in range(ne):
        rank = jnp.zeros_like(best)
        for k in range(ne):
            if k < i:
                rank = rank + jnp.where(cs[k] >= cs[i], 1.0, 0.0)
            elif k > i:
                rank = rank + jnp.where(cs[k] > cs[i], 1.0, 0.0)
        gates.append(jnp.where(rank < 1.5, ss[i], 0.0))
    inv = 1.0 / sum(gates)
    rows = [gates[i] * inv * gsel[g] for i in range(ne) for g in range(ng)]
    wt = jnp.concatenate(rows + [jnp.zeros((LANE - N_EXPERTS, tm), F32)], axis=0)
    wcol = wt.T

    acc = jnp.zeros((tm, x_ref.shape[1]), F32)
    for e in range(N_EXPERTS):
        col = ng * (e % ne) + e // ne
        gu = _dot(xb, wgu_ref[e])
        hh = jax.nn.silu(gu[:, :D_EXPERT]) * gu[:, D_EXPERT:] * wcol[:, col:col + 1]
        acc = acc + _dot(hh.astype(BF16), wd_ref[e])
    o_ref[...] = _layer_norm(ALPHA * x1 + acc, l2g_ref[...], l2b_ref[...])


def _post(x, olat, ogla, oml, lw, rw, rb, tm, precise):
    n, d = x.shape
    row = lambda i: (i, 0)
    single = dict(pipeline_mode=pl.Buffered(1))
    rw = rw if precise else rw[:d]
    consts = [lw["wo"], lw["wo2_3"] if precise else lw["wo2"], lw["ln1_g"], lw["ln1_b"], rw, rb, lw["wgu"], lw["wd"], lw["ln2_g"], lw["ln2_b"]]
    in_specs = [pl.BlockSpec((tm, d), row), pl.BlockSpec((tm, 1024), row), pl.BlockSpec((tm, 256), row),
                pl.BlockSpec((tm, 256), row)]
    for w in consts:
        nd = w.ndim
        in_specs.append(pl.BlockSpec(w.shape, (lambda i, _nd=nd: (0,) * _nd), **single))
    return pl.pallas_call(
        _post_kernel,
        grid=(n // tm,),
        in_specs=in_specs,
        out_specs=pl.BlockSpec((tm, d), row),
        out_shape=jax.ShapeDtypeStruct((n, d), F32),
        compiler_params=_params(("parallel",)),
        name="post",
    )(x, olat, ogla, oml, *consts)


def _rope_tables(pos):
    half = MLA_ROPE // 2
    inv = ROPE_THETA ** (-jnp.arange(half, dtype=F32) / half)
    ang = pos.astype(F32)[:, None] * inv
    cos, sin = jnp.cos(ang), jnp.sin(ang)
    c2 = jnp.concatenate([cos, cos], axis=1)
    s2 = jnp.concatenate([-sin, sin], axis=1)
    cq = jnp.tile(c2, (1, MLA_HEADS))
    sq = jnp.tile(s2, (1, MLA_HEADS))
    kt = jnp.concatenate([c2, s2, jnp.zeros((pos.shape[0], LANE - 2 * MLA_ROPE), F32)], axis=1)
    return cq, sq, kt


def _prep_layer(l, p):
    d = p["w_in"].shape[1]
    offs = np.concatenate([[0], np.cumsum(IN_SIZES)])
    wi = p["w_in"][l]
    cq, ckv, kr, gq, gk, gv, ga, gr, mu, mv, mi, mf, mo = [wi[:, offs[i]:offs[i + 1]] for i in range(13)]
    half = MLA_ROPE // 2
    kr_sw = jnp.concatenate([kr[:, half:], kr[:, :half]], axis=1)
    misc = jnp.concatenate([kr, kr_sw, ga, mi, mf, jnp.zeros((d, LANE - MS_MF - ML_HEADS), F32)], axis=1)
    w_in32 = jnp.concatenate([cq, gv, gr, mu, mv, mo, ckv, gq, gk, misc], axis=1)
    w_in = w_in32.astype(BF16)
    w_in3 = _split_weight(w_in32) if l == 0 else None

    dq = MLA_NOPE + MLA_ROPE
    wuq = p["mla_w_uq"][l].reshape(MLA_Q_RANK, MLA_HEADS, dq)
    wukv = p["mla_w_ukv"][l].reshape(MLA_KV_RANK, MLA_HEADS, MLA_NOPE + MLA_V)
    wuk_t = jnp.transpose(wukv[..., :MLA_NOPE], (1, 2, 0))
    wql = _fold(jnp.transpose(wuq[..., :MLA_NOPE], (1, 0, 2)), wuk_t)
    wql = jnp.transpose(wql, (1, 0, 2)).reshape(MLA_Q_RANK, MLA_HEADS * MLA_KV_RANK)
    wqr = wuq[..., MLA_NOPE:]
    wqr_sw = jnp.concatenate([wqr[..., half:], wqr[..., :half]], axis=-1)
    wq = jnp.concatenate([wql, wqr.reshape(MLA_Q_RANK, -1), wqr_sw.reshape(MLA_Q_RANK, -1)], axis=1).astype(BF16)

    wout = p["w_out"][l]
    nv = MLA_HEADS * MLA_V
    wuv = jnp.transpose(wukv[..., MLA_NOPE:], (1, 0, 2))
    wo_mla = _fold(wuv, wout[:nv].reshape(MLA_HEADS, MLA_V, d)).reshape(MLA_HEADS * MLA_KV_RANK, d)
    wo = wo_mla.astype(BF16)
    wo2 = wout[nv:].astype(BF16)
    wo2_3 = _split_weight(wout[nv:]) if l == 0 else None

    wa = jnp.zeros((LANE, LANE), F32).at[MS_GA:MS_GA + GLA_GATE_RANK].set(p["gla_w_a"][l])
    gbias = (jnp.zeros((1, LANE), F32).at[0, MS_MI:MS_MI + ML_HEADS].set(p["ml_b_i"][l])
             .at[0, MS_MF:MS_MF + ML_HEADS].set(p["ml_b_f"][l]))

    def block_diag(w):
        out = jnp.zeros((ML_WIDTH, ML_WIDTH), F32)
        for h in range(ML_HEADS):
            out = out.at[h * ML_DH:(h + 1) * ML_DH, h * ML_DH:(h + 1) * ML_DH].set(w[h])
        return out

    conv_w = jnp.concatenate([p["ml_conv_w"][l], jnp.zeros((8 - CONV_W, ML_WIDTH), F32)], axis=0)
    wgu = jnp.concatenate([p["moe_w_gate"][l], p["moe_w_up"][l]], axis=2).astype(BF16)
    return dict(
        w_in=w_in, w_in3=w_in3, wq=wq, wo=wo, wo2=wo2, wo2_3=wo2_3, wa=wa, gbias=gbias,
        qn=p["mla_q_norm"][l][None], kvn=p["mla_kv_norm"][l][None],
        ba=p["gla_b_a"][l][None], gnorm=jnp.tile(p["gla_norm"][l], GLA_HEADS)[None],
        conv_w=conv_w, conv_b=p["ml_conv_b"][l][None], wq_bd=block_diag(p["ml_w_q"][l]),
        wk_bd=block_diag(p["ml_w_k"][l]), mnorm=p["ml_norm"][l][None], mskip=p["ml_skip"][l][None],
        ln1_g=p["ln1_g"][l][None], ln1_b=p["ln1_b"][l][None], ln2_g=p["ln2_g"][l][None], ln2_b=p["ln2_b"][l][None],
        wgu=wgu, wd=p["moe_w_down"][l].astype(BF16),
    )


def _gla_state_out(st):
    n = st.shape[0]
    s5 = st.reshape(n, GLA_HEADS, GLA_DV, GLA_HEADS, GLA_DK)
    return jnp.stack([jnp.transpose(s5[:, h, :, h, :], (0, 2, 1)) for h in range(GLA_HEADS)], axis=1)


def _gla_state_in(s):
    n = s.shape[0]
    out = jnp.zeros((n, GLA_HEADS, GLA_DV, GLA_HEADS, GLA_DK), F32)
    for h in range(GLA_HEADS):
        out = out.at[:, h, :, h, :].set(jnp.transpose(s[:, h], (0, 2, 1)))
    return out.reshape(n, GLA_HEADS * GLA_DV, GLA_HEADS * GLA_DK)


def _ml_state_out(ct):
    n = ct.shape[0]
    c5 = ct.reshape(n, ML_HEADS, ML_DH, ML_HEADS, ML_DH)
    return jnp.stack([jnp.transpose(c5[:, h, :, h, :], (0, 2, 1)) for h in range(ML_HEADS)], axis=1)


def _ml_state_in(cs):
    n = cs.shape[0]
    out = jnp.zeros((n, ML_HEADS, ML_DH, ML_HEADS, ML_DH), F32)
    for h in range(ML_HEADS):
        out = out.at[:, h, :, h, :].set(jnp.transpose(cs[:, h], (0, 2, 1)))
    return out.reshape(n, ML_WIDTH, ML_WIDTH)


def _pick_tile(n, cands):
    for t in cands:
        if n % t == 0:
            return t
    raise ValueError(f"no row tile for {n} rows")


def kernel(x_prompt, x_sample, cache_kv_latent, cache_k_rope, state_gla, state_mlstm_C, state_mlstm_n, state_mlstm_m, state_mlstm_conv, page_table, meta_tokens, ln_in_g, ln_in_b, w_in, mla_q_norm, mla_w_uq, mla_kv_norm, mla_w_ukv, gla_w_a, gla_b_a, gla_norm, ml_conv_w, ml_conv_b, ml_w_q, ml_w_k, ml_b_i, ml_b_f, ml_norm, ml_skip, w_out, ln1_g, ln1_b, ln2_g, ln2_b, router_w, router_bias, moe_w_gate, moe_w_up, moe_w_down):
    p = dict(w_in=w_in, mla_q_norm=mla_q_norm, mla_w_uq=mla_w_uq, mla_kv_norm=mla_kv_norm, mla_w_ukv=mla_w_ukv,
             gla_w_a=gla_w_a, gla_b_a=gla_b_a, gla_norm=gla_norm, ml_conv_w=ml_conv_w, ml_conv_b=ml_conv_b,
             ml_w_q=ml_w_q, ml_w_k=ml_w_k, ml_b_i=ml_b_i, ml_b_f=ml_b_f, ml_norm=ml_norm, ml_skip=ml_skip,
             w_out=w_out, ln1_g=ln1_g, ln1_b=ln1_b, ln2_g=ln2_g, ln2_b=ln2_b, moe_w_gate=moe_w_gate,
             moe_w_up=moe_w_up, moe_w_down=moe_w_down)
    depth = w_in.shape[0]
    nb, seq, d = x_prompt.shape
    ns, t_new, _ = x_sample.shape
    n_pages = page_table.shape[1]
    page = cache_kv_latent.shape[2]
    past_len = n_pages * page
    lp = PAD + seq
    assert seq % CHUNK == 0 and t_new == 8 and t_new >= CONV_W - 1

    layers = [_prep_layer(l, p) for l in range(depth)]
    consts = dict(mats=jnp.asarray(_level_mats(), BF16))
    perm = np.array([E_PER_GROUP * g + i for i in range(E_PER_GROUP) for g in range(N_GROUPS)])
    rw_hi, rw_mid = _split2_bits(router_w.astype(F32)[:, perm])
    rw = jnp.pad(jnp.concatenate([rw_hi, rw_hi, rw_mid], axis=0), ((0, 0), (0, LANE - N_EXPERTS)))
    rb = router_bias.astype(F32)[perm][:, None]
    g_in, b_in = ln_in_g[None], ln_in_b[None]

    meta = jnp.broadcast_to(meta_tokens[None].astype(F32), (nb, N_META, d))
    xp = jnp.concatenate([jnp.zeros((nb, PAD - N_META, d), F32), meta, x_prompt], axis=1).reshape(nb * lp, d)
    tabs_p = _rope_tables(jnp.arange(lp) - (PAD - N_META))
    tm_p = _pick_tile(nb * lp, (512, 384, 256, 128))
    tk_p = _pick_tile(lp, (384, 128))
    p_out = dict(kv=[], kr=[], gla=[], C=[], n=[], m=[], conv=[])
    for l in range(depth):
        lw = layers[l]
        precise = l == 0
        proj, xp = _in_proj(xp, g_in, b_in, lw["w_in3"] if precise else lw["w_in"], l == 0, tm_p)
        qlat, qrope, ckvn, krope, kcat, vt = _mla_prep(proj, tabs_p, lw["qn"], lw["kvn"], lw["wq"], tk_p, lp // tk_p,
                                                       MLA_SCALE * LOG2E)
        olat = _attn_prompt(qlat, qrope, kcat, vt, nb, lp, CHUNK, tk_p)
        ogla, oml, st, ct, nst, mst = _rec(proj, lw, consts, nb, lp // CHUNK, CHUNK, PAD - N_META, lp, None, precise)
        xp = _post(xp, olat, ogla, oml, lw, rw, rb, tm_p, precise)
        lo = PAD - N_META
        p_out["kv"].append(ckvn.reshape(nb, lp, -1)[:, lo:])
        p_out["kr"].append(krope.reshape(nb, lp, -1)[:, lo:])
        p_out["gla"].append(_gla_state_out(st))
        p_out["C"].append(_ml_state_out(ct))
        p_out["n"].append(nst[:, 0].reshape(nb, ML_HEADS, ML_DH))
        p_out["m"].append(mst[:, 0, :ML_HEADS])
        mu_cols = proj[:, PJ_MU * 256:(PJ_MU + 1) * 256].reshape(nb, lp, ML_WIDTH)
        p_out["conv"].append(mu_cols[:, lp - (CONV_W - 1):])
    y_prompt = xp.reshape(nb, lp, d)[:, PAD:]

    xs = x_sample.reshape(ns * t_new, d)
    tabs_s = _rope_tables(past_len + jnp.arange(t_new))
    tm_s = _pick_tile(ns * t_new, (512, 256, 128, 64, 32, 16))
    tabs_s = tuple(jnp.tile(t, (tm_s // t_new, 1)) for t in tabs_s)
    cache_rt = jnp.swapaxes(cache_k_rope, 2, 3)
    s_out = dict(kv=[], kr=[], gla=[], C=[], n=[], m=[], conv=[])
    for l in range(depth):
        lw = layers[l]
        precise = l == 0
        proj, xs = _in_proj(xs, g_in, b_in, lw["w_in3"] if precise else lw["w_in"], l == 0, tm_s)
        qlat, qrope, ckvn, krope, kcat, _ = _mla_prep(proj, tabs_s, lw["qn"], lw["kvn"], lw["wq"], tm_s, 1, MLA_SCALE)
        olat = _attn_sample(qlat, qrope, kcat, cache_kv_latent, cache_rt, page_table, t_new, l)
        ns0 = jnp.broadcast_to(state_mlstm_n[l].reshape(ns, 1, ML_WIDTH), (ns, 8, ML_WIDTH))
        ms0 = jnp.broadcast_to(jnp.pad(state_mlstm_m[l], ((0, 0), (0, LANE - ML_HEADS)))[:, None], (ns, 8, LANE))
        cv0 = jnp.pad(state_mlstm_conv[l], ((0, 0), (8 - (CONV_W - 1), 0), (0, 0)))
        past = (_gla_state_in(state_gla[l]), _ml_state_in(state_mlstm_C[l]), ns0, ms0, cv0)
        ogla, oml, st, ct, nst, mst = _rec(proj, lw, consts, ns, 1, t_new, 0, t_new, past, precise)
        xs = _post(xs, olat, ogla, oml, lw, rw, rb, tm_s, precise)
        s_out["kv"].append(ckvn.reshape(ns, t_new, -1))
        s_out["kr"].append(krope.reshape(ns, t_new, -1))
        s_out["gla"].append(_gla_state_out(st))
        s_out["C"].append(_ml_state_out(ct))
        s_out["n"].append(nst[:, 0].reshape(ns, ML_HEADS, ML_DH))
        s_out["m"].append(mst[:, 0, :ML_HEADS])
        mu_cols = proj[:, PJ_MU * 256:(PJ_MU + 1) * 256].reshape(ns, t_new, ML_WIDTH)
        s_out["conv"].append(mu_cols[:, t_new - (CONV_W - 1):])
    y_sample = xs.reshape(ns, t_new, d)

    st_p = [jnp.stack(p_out[k]) for k in ("kv", "kr", "gla", "C", "n", "m", "conv")]
    st_s = [jnp.stack(s_out[k]) for k in ("kv", "kr", "gla", "C", "n", "m", "conv")]
    return (y_prompt, y_sample, *st_p, *st_s)
```

```python
import functools

import numpy as np
import jax
import jax.numpy as jnp
from jax import lax
from jax.experimental import pallas as pl
from jax.experimental.pallas import tpu as pltpu

F32 = jnp.float32
BF16 = jnp.bfloat16

N_META = 16
MLA_HEADS = 8
MLA_Q_RANK = 256
MLA_KV_RANK = 128
MLA_NOPE = 64
MLA_ROPE = 32
MLA_V = 64
MLA_SCALE = (MLA_NOPE + MLA_ROPE) ** -0.5
LOG2E = float(np.log2(np.e))
ROPE_THETA = 10000.0
GLA_HEADS = 4
GLA_DK = 32
GLA_DV = 64
GLA_GATE_RANK = 16
GLA_TAU = 16.0
ML_HEADS = 4
ML_DH = 64
ML_WIDTH = ML_HEADS * ML_DH
CONV_W = 4
N_EXPERTS = 16
N_GROUPS = 4
E_PER_GROUP = N_EXPERTS // N_GROUPS
D_EXPERT = 256
DEPTH = 2
ALPHA = (2 * DEPTH) ** 0.25
LN_EPS = 1e-5
RMS_EPS = 1e-6
IN_SIZES = (MLA_Q_RANK, MLA_KV_RANK, MLA_ROPE,
            GLA_HEADS * GLA_DK, GLA_HEADS * GLA_DK, GLA_HEADS * GLA_DV, GLA_GATE_RANK, GLA_HEADS * GLA_DV,
            ML_WIDTH, ML_WIDTH, ML_HEADS, ML_HEADS, ML_WIDTH)

LANE = 128
CHUNK = 128
PAD = 128
D_PROJ = 2048
PJ_CQ, PJ_GV, PJ_GR, PJ_MU, PJ_MV, PJ_MO = 0, 1, 2, 3, 4, 5
PJ_CKV, PJ_GQ, PJ_GK, PJ_MISC = 12, 13, 14, 15
MS_KR, MS_KRSW, MS_GA, MS_MI, MS_MF = 0, 32, 64, 80, 84
NEG_BIG = -1e30
VMEM_LIMIT = 56 * 1024 * 1024

NT_DIMS = (((1,), (1,)), ((), ()))
TN_DIMS = (((0,), (0,)), ((), ()))


def _dot(a, b):
    return jnp.dot(a, b, preferred_element_type=F32)


def _dot_nt(a, b):
    return lax.dot_general(a, b, NT_DIMS, preferred_element_type=F32)


def _dot_tn(a, b):
    return lax.dot_general(a, b, TN_DIMS, preferred_element_type=F32)


NN_DIMS = (((1,), (0,)), ((), ()))


def _mm(a, b, dims, precise):
    ah, bh = a.astype(BF16), b.astype(BF16)
    out = lax.dot_general(ah, bh, dims, preferred_element_type=F32)
    if precise:
        am = (a - ah.astype(F32)).astype(BF16)
        bm = (b - bh.astype(F32)).astype(BF16)
        out = (out + lax.dot_general(am, bh, dims, preferred_element_type=F32)
               + lax.dot_general(ah, bm, dims, preferred_element_type=F32))
    return out


def _split3(x):
    hi = x.astype(BF16)
    r = x - hi.astype(F32)
    mid = r.astype(BF16)
    lo = (r - mid.astype(F32)).astype(BF16)
    return hi, mid, lo


def _dot_exact_lhs(m01, x):
    hi, mid, lo = _split3(x)
    return _dot(m01, hi) + _dot(m01, mid) + _dot(m01, lo)


def _layer_norm(x, g, b):
    mu = jnp.mean(x, axis=-1, keepdims=True)
    xc = x - mu
    var = jnp.mean(xc * xc, axis=-1, keepdims=True)
    return xc * lax.rsqrt(var + LN_EPS) * g + b


def _params(sem):
    return pltpu.CompilerParams(dimension_semantics=sem, vmem_limit_bytes=VMEM_LIMIT)


def _const_spec(shape):
    nd = len(shape)
    return pl.BlockSpec(shape, lambda *_: (0,) * nd)


def _fold_kernel(a_ref, b_ref, o_ref):
    o_ref[0] = jnp.dot(a_ref[0], b_ref[0], preferred_element_type=F32, precision=lax.Precision.HIGHEST)


def _fold(a, b):
    h, m, k = a.shape
    n = b.shape[2]
    return pl.pallas_call(
        _fold_kernel,
        grid=(h,),
        in_specs=[pl.BlockSpec((1, m, k), lambda i: (i, 0, 0)), pl.BlockSpec((1, k, n), lambda i: (i, 0, 0))],
        out_specs=pl.BlockSpec((1, m, n), lambda i: (i, 0, 0)),
        out_shape=jax.ShapeDtypeStruct((h, m, n), F32),
        compiler_params=_params(("arbitrary",)),
        name="fold",
    )(a, b)


def _in_proj_kernel(x_ref, g_ref, b_ref, w_ref, proj_ref, *xn_refs, ln_in):
    x = x_ref[...]
    if ln_in:
        x = _layer_norm(x, g_ref[...], b_ref[...])
        xn_refs[0][...] = x
    xb = x.astype(BF16)
    if w_ref.shape[0] == 3 * x.shape[1]:
        x_mid = (x - xb.astype(F32)).astype(BF16)
        xb = jnp.concatenate([xb, x_mid, xb], axis=1)
    proj_ref[...] = _dot(xb, w_ref[...])


def _split2_bits(w):
    bits = lax.bitcast_convert_type(w.astype(F32), jnp.uint32) & jnp.uint32(0xFFFF0000)
    hi = lax.bitcast_convert_type(bits, F32)
    return hi.astype(BF16), (w - hi).astype(BF16)


def _split_weight(w):
    hi, mid = _split2_bits(w)
    return jnp.concatenate([hi, hi, mid], axis=0)


def _in_proj(x, g, b, w, ln_in, tm):
    n, d = x.shape
    out_shape = [jax.ShapeDtypeStruct((n, D_PROJ), F32)]
    out_specs = [pl.BlockSpec((tm, D_PROJ), lambda i: (i, 0))]
    if ln_in:
        out_shape.append(jax.ShapeDtypeStruct((n, d), F32))
        out_specs.append(pl.BlockSpec((tm, d), lambda i: (i, 0)))
    res = pl.pallas_call(
        functools.partial(_in_proj_kernel, ln_in=ln_in),
        grid=(n // tm,),
        in_specs=[pl.BlockSpec((tm, d), lambda i: (i, 0)), _const_spec((1, d)), _const_spec((1, d)),
                  pl.BlockSpec(w.shape, lambda i: (0, 0), pipeline_mode=pl.Buffered(1))],
        out_specs=out_specs,
        out_shape=out_shape,
        compiler_params=_params(("parallel",)),
        name="in_proj",
    )(x, g, b, w)
    return (res[0], res[1]) if ln_in else (res[0], x)


def _mla_prep_kernel(cq_ref, ckv_ref, misc_ref, cq_tab, sq_tab, k_tab, qn_ref, kvn_ref, wq_ref,
                     qlat_ref, qrope_ref, ckvn_ref, krope_ref, kcat_ref, vt_ref, *, qscale):
    cq = cq_ref[...]
    cqn = cq * lax.rsqrt(jnp.mean(cq * cq, axis=-1, keepdims=True) + RMS_EPS) * qn_ref[...]
    q = _dot(cqn.astype(BF16), wq_ref[...])
    nl = MLA_HEADS * MLA_KV_RANK
    nr = MLA_HEADS * MLA_ROPE
    qlat_ref[...] = (q[:, :nl] * qscale).astype(BF16)
    rot = q[:, nl:nl + nr] * cq_tab[...] + q[:, nl + nr:] * sq_tab[...]
    qrope_ref[...] = (rot * qscale).astype(BF16)
    ckv = ckv_ref[...]
    ckvn = ckv * lax.rsqrt(jnp.mean(ckv * ckv, axis=-1, keepdims=True) + RMS_EPS) * kvn_ref[...]
    ckvn_ref[...] = ckvn
    p = misc_ref[...] * k_tab[...]
    kr = p[:, MS_KR:MS_KR + MLA_ROPE] + p[:, MS_KRSW:MS_KRSW + MLA_ROPE]
    krope_ref[...] = kr
    kcat_ref[...] = jnp.concatenate([ckvn, kr], axis=1).astype(BF16)
    vt_ref[0] = ckvn.T.astype(BF16)


def _mla_prep(proj, tabs, qn, kvn, wq, tp, tab_blocks, qscale):
    n = proj.shape[0]
    cq_tab, sq_tab, k_tab = tabs
    nq = wq.shape[1]
    dk = MLA_KV_RANK + MLA_ROPE
    row = lambda c: (lambda i: (i, c))
    tab = lambda i: (i % tab_blocks, 0)
    return pl.pallas_call(
        functools.partial(_mla_prep_kernel, qscale=qscale),
        grid=(n // tp,),
        in_specs=[pl.BlockSpec((tp, 256), row(PJ_CQ)), pl.BlockSpec((tp, LANE), row(PJ_CKV)),
                  pl.BlockSpec((tp, LANE), row(PJ_MISC)),
                  pl.BlockSpec((tp, 256), tab), pl.BlockSpec((tp, 256), tab), pl.BlockSpec((tp, LANE), tab),
                  _const_spec((1, MLA_Q_RANK)), _const_spec((1, MLA_KV_RANK)), _const_spec((MLA_Q_RANK, nq))],
        out_specs=[pl.BlockSpec((tp, 1024), row(0)), pl.BlockSpec((tp, 256), row(0)),
                   pl.BlockSpec((tp, MLA_KV_RANK), row(0)), pl.BlockSpec((tp, MLA_ROPE), row(0)),
                   pl.BlockSpec((tp, dk), row(0)), pl.BlockSpec((1, MLA_KV_RANK, tp), lambda i: (i, 0, 0))],
        out_shape=[jax.ShapeDtypeStruct((n, 1024), BF16), jax.ShapeDtypeStruct((n, 256), BF16),
                   jax.ShapeDtypeStruct((n, MLA_KV_RANK), F32), jax.ShapeDtypeStruct((n, MLA_ROPE), F32),
                   jax.ShapeDtypeStruct((n, dk), BF16), jax.ShapeDtypeStruct((n // tp, MLA_KV_RANK, tp), BF16)],
        compiler_params=_params(("parallel",)),
        name="mla_prep",
    )(proj, proj, proj, cq_tab, sq_tab, k_tab, qn, kvn, wq)


def _attn_prompt_kernel(ql_ref, qr_ref, k_ref, vt_ref, o_ref, m_ref, l_ref, acc_ref, sa_ref, sb_ref, *, tq, tk,
                        lo_key):
    qi = pl.program_id(1)
    m_rows = tq * MLA_HEADS
    ql = ql_ref[...]
    qr = qr_ref[...]
    q = jnp.concatenate(
        [jnp.concatenate([ql[:, h * MLA_KV_RANK:(h + 1) * MLA_KV_RANK], qr[:, h * MLA_ROPE:(h + 1) * MLA_ROPE]], axis=1)
         for h in range(MLA_HEADS)], axis=0)
    qpos = qi * tq + lax.broadcasted_iota(jnp.int32, (1, m_rows), 1) % tq
    lowk = jnp.where(qpos < lo_key, 0, lo_key)

    def scores(j):
        start = j * tk if isinstance(j, int) else pl.multiple_of(j * tk, tk)
        return _dot_nt(k_ref[pl.ds(start, tk), :], q)

    def consume(s, j, masked, first):
        if masked:
            kpos = j * tk + lax.broadcasted_iota(jnp.int32, (tk, 1), 0)
            s = jnp.where(kpos <= qpos, jnp.where(kpos >= lowk, s, NEG_BIG), NEG_BIG)
        smax = jnp.max(s, axis=0, keepdims=True)
        if first:
            m_new = smax
            p = jnp.exp2(s - m_new)
            l_ref[...] = jnp.sum(p, axis=0, keepdims=True)
            acc_ref[...] = _dot(vt_ref[j], p.astype(BF16))
        else:
            m_old = m_ref[...]
            m_new = jnp.maximum(m_old, smax)
            a = jnp.exp2(m_old - m_new)
            p = jnp.exp2(s - m_new)
            l_ref[...] = l_ref[...] * a + jnp.sum(p, axis=0, keepdims=True)
            acc_ref[...] = acc_ref[...] * a + _dot(vt_ref[j], p.astype(BF16))
        m_ref[...] = m_new

    jd = (qi * tq) // tk
    consume(scores(0), 0, True, True)

    n_mid = jnp.maximum(jd - 1, 0)
    nk = vt_ref.shape[0]

    @pl.when(n_mid > 0)
    def _():
        sa_ref[...] = scores(jnp.minimum(1, nk - 1))

    def pair(i, c):
        t1 = 1 + 2 * i
        sb_ref[...] = scores(t1 + 1)
        consume(sa_ref[...], t1, False, False)
        sa_ref[...] = scores(jnp.minimum(t1 + 2, nk - 1))
        consume(sb_ref[...], t1 + 1, False, False)
        return c

    lax.fori_loop(0, n_mid // 2, pair, 0)

    @pl.when(n_mid % 2 == 1)
    def _():
        consume(sa_ref[...], jd - 1, False, False)

    @pl.when(jd > 0)
    def _():
        consume(scores(jd), jd, True, False)

    o = acc_ref[...] / l_ref[...]
    for h in range(MLA_HEADS):
        o_ref[:, h * MLA_KV_RANK:(h + 1) * MLA_KV_RANK] = o[:, h * tq:(h + 1) * tq].T.astype(BF16)


def _attn_prompt(qlat, qrope, kcat, vt, n_seq, lp, tq, tk):
    m_rows = tq * MLA_HEADS
    nq = lp // tq
    nk = lp // tk
    dk = MLA_KV_RANK + MLA_ROPE
    qmap = lambda b, i: (b * nq + i, 0)
    return pl.pallas_call(
        functools.partial(_attn_prompt_kernel, tq=tq, tk=tk, lo_key=PAD - N_META),
        grid=(n_seq, nq),
        in_specs=[pl.BlockSpec((tq, MLA_HEADS * MLA_KV_RANK), qmap), pl.BlockSpec((tq, MLA_HEADS * MLA_ROPE), qmap),
                  pl.BlockSpec((lp, dk), lambda b, i: (b, 0)),
                  pl.BlockSpec((nk, MLA_KV_RANK, tk), lambda b, i: (b, 0, 0))],
        out_specs=pl.BlockSpec((tq, MLA_HEADS * MLA_KV_RANK), qmap),
        out_shape=jax.ShapeDtypeStruct(qlat.shape, BF16),
        scratch_shapes=[pltpu.VMEM((1, m_rows), F32), pltpu.VMEM((1, m_rows), F32),
                        pltpu.VMEM((MLA_KV_RANK, m_rows), F32),
                        pltpu.VMEM((tk, m_rows), F32), pltpu.VMEM((tk, m_rows), F32)],
        compiler_params=_params(("parallel", "arbitrary")),
        name="attn_prompt",
    )(qlat, qrope, kcat, vt)


def _attn_sample_kernel(pt_ref, ql_ref, qr_ref, knew_ref, cache_c, cache_r, o_ref,
                        bufc, bufr, semc, semr, m_ref, l_ref, acc_ref, *, n_pages, gp, t_new, page, layer):
    s_id = pl.program_id(0)
    n_seq = pl.num_programs(0)
    ng = n_pages // gp
    gk = gp * page

    def copies(seq, g, slot):
        out = []
        for p in range(gp):
            pg = pt_ref[seq * n_pages + g * gp + p]
            out.append(pltpu.make_async_copy(cache_c.at[layer, pg], bufc.at[slot, pl.ds(p * page, page)],
                                             semc.at[slot]))
            out.append(pltpu.make_async_copy(cache_r.at[layer, pg], bufr.at[slot, p], semr.at[slot]))
        return out

    @pl.when(s_id == 0)
    def _():
        for c in copies(0, 0, 0):
            c.start()

    ql = ql_ref[...]
    qr = qr_ref[...]
    q = jnp.concatenate([ql, qr], axis=1)
    m_ref[...] = jnp.full(m_ref.shape, NEG_BIG, F32)
    l_ref[...] = jnp.zeros(l_ref.shape, F32)
    acc_ref[...] = jnp.zeros(acc_ref.shape, F32)

    def update(s, v):
        m_old = m_ref[...]
        m_new = jnp.maximum(m_old, jnp.max(s, axis=-1, keepdims=True))
        a = jnp.exp(m_old - m_new)
        p = jnp.exp(s - m_new)
        l_ref[...] = l_ref[...] * a + jnp.sum(p, axis=-1, keepdims=True)
        acc_ref[...] = acc_ref[...] * a + _dot(p.astype(BF16), v)
        m_ref[...] = m_new

    def group(g, c):
        it = s_id * ng + g
        slot = it % 2
        nxt_seq = jnp.where(g + 1 < ng, s_id, s_id + 1)
        nxt_g = jnp.where(g + 1 < ng, g + 1, 0)

        @pl.when(nxt_seq < n_seq)
        def _():
            for cp in copies(nxt_seq, nxt_g, 1 - slot):
                cp.start()

        for cp in copies(s_id, g, slot):
            cp.wait()
        n_half = 2 if gp % 2 == 0 else 1
        hp = gp // n_half
        halves = []
        for i in range(n_half):
            kc = bufc[slot, pl.ds(i * hp * page, hp * page)].astype(BF16)
            krt = jnp.concatenate([bufr[slot, p] for p in range(i * hp, (i + 1) * hp)], axis=1).astype(BF16)
            halves.append((_dot_nt(ql, kc) + _dot(qr, krt), kc))
        for s, kc in halves:
            update(s, kc)
        return c

    lax.fori_loop(0, ng, group, 0)

    kn = knew_ref[...]
    s = _dot_nt(q, kn)
    rows = q.shape[0]
    qt = lax.broadcasted_iota(jnp.int32, (rows, t_new), 0) // MLA_HEADS
    kt = lax.broadcasted_iota(jnp.int32, (rows, t_new), 1)
    s = jnp.where(kt <= qt, s, NEG_BIG)
    update(s, kn[:, :MLA_KV_RANK])
    o_ref[...] = (acc_ref[...] / l_ref[...]).astype(BF16)


def _attn_sample(qlat, qrope, kcat, cache_c, cache_r, page_table, t_new, layer):
    n_seq, n_pages = page_table.shape
    page = cache_c.shape[2]
    gp = _pick_tile(n_pages, (64, 32, 16, 8, 4, 2, 1))
    rows = t_new * MLA_HEADS
    dk = MLA_KV_RANK + MLA_ROPE
    ql = qlat.reshape(-1, MLA_KV_RANK)
    qr = qrope.reshape(-1, MLA_ROPE)
    grid_spec = pltpu.PrefetchScalarGridSpec(
        num_scalar_prefetch=1,
        grid=(n_seq,),
        in_specs=[pl.BlockSpec((rows, MLA_KV_RANK), lambda s, pt: (s, 0)),
                  pl.BlockSpec((rows, MLA_ROPE), lambda s, pt: (s, 0)),
                  pl.BlockSpec((t_new, dk), lambda s, pt: (s, 0)),
                  pl.BlockSpec(memory_space=pl.ANY), pl.BlockSpec(memory_space=pl.ANY)],
        out_specs=pl.BlockSpec((rows, MLA_KV_RANK), lambda s, pt: (s, 0)),
        scratch_shapes=[pltpu.VMEM((2, gp * page, MLA_KV_RANK), F32), pltpu.VMEM((2, gp, MLA_ROPE, page), F32),
                        pltpu.SemaphoreType.DMA((2,)), pltpu.SemaphoreType.DMA((2,)),
                        pltpu.VMEM((rows, 1), F32), pltpu.VMEM((rows, 1), F32),
                        pltpu.VMEM((rows, MLA_KV_RANK), F32)],
    )
    out = pl.pallas_call(
        functools.partial(_attn_sample_kernel, n_pages=n_pages, gp=gp, t_new=t_new, page=page, layer=layer),
        grid_spec=grid_spec,
        out_shape=jax.ShapeDtypeStruct(ql.shape, BF16),
        compiler_params=_params(("arbitrary",)),
        name="attn_sample",
    )(page_table.reshape(-1), ql, qr, kcat, cache_c, cache_r)
    return out.reshape(-1, MLA_HEADS * MLA_KV_RANK)


def _level_mats():
    c = CHUNK
    t = np.arange(c)[:, None]
    s = np.arange(c)[None, :]
    mats = [(s <= t)]
    h = c // 2
    while h >= 1:
        start = (t // (2 * h)) * (2 * h) + h
        second = (t % (2 * h)) >= h
        mats.append(np.where(second, (s >= start) & (s <= t), (s > t) & (s < start)))
        h //= 2
    return np.concatenate(mats, axis=0).astype(np.float32)


N_LEVELS = int(np.log2(CHUNK))


def _rec_kernel(*refs, rows, lo_row, hi_row, has_past, precise):
    (gq_ref, gk_ref, gv_ref, gr_ref, mu_ref, mv_ref, mo_ref, misc_ref,
     mats_ref, wa_ref, ba_ref, gnorm_ref, cw_ref, cb_ref, wq_ref, wk_ref, gbias_ref, mnorm_ref, mskip_ref) = refs[:19]
    k = 19
    if has_past:
        st0_ref, ct0_ref, ns0_ref, ms0_ref, cv0_ref = refs[k:k + 5]
        k += 5
    ogla_ref, oml_ref, st_ref, ct_ref, ns_ref, ms_ref, prev_ref = refs[k:k + 7]
    j = pl.program_id(1)
    c = CHUNK
    mm = functools.partial(_mm, precise=precise)

    @pl.when(j == 0)
    def _():
        if has_past:
            st_ref[0] = st0_ref[0]
            ct_ref[0] = ct0_ref[0]
            ns_ref[0] = ns0_ref[0]
            ms_ref[0] = ms0_ref[0]
            prev_ref[...] = cv0_ref[0]
        else:
            st_ref[...] = jnp.zeros(st_ref.shape, F32)
            ct_ref[...] = jnp.zeros(ct_ref.shape, F32)
            ns_ref[...] = jnp.zeros(ns_ref.shape, F32)
            ms_ref[...] = jnp.zeros(ms_ref.shape, F32)
            prev_ref[...] = jnp.zeros(prev_ref.shape, F32)

    def load(ref):
        x = ref[...]
        if rows < c:
            x = jnp.concatenate([x, jnp.zeros((c - rows, x.shape[1]), F32)], axis=0)
        return x

    row_i = lax.broadcasted_iota(jnp.int32, (c, 1), 0)
    grow = j * c + row_i
    valid = (grow >= lo_row) & (grow < hi_row)
    lane128 = lax.broadcasted_iota(jnp.int32, (1, LANE), 1)
    lane256 = lax.broadcasted_iota(jnp.int32, (1, 256), 1)
    t_i = lax.broadcasted_iota(jnp.int32, (c, c), 0)
    s_i = lax.broadcasted_iota(jnp.int32, (c, c), 1)
    causal = s_i <= t_i
    mats = mats_ref[...]
    tri = mats[:c]
    misc = load(misc_ref)

    la = jax.nn.log_sigmoid(mm(misc, wa_ref[...], NN_DIMS) + ba_ref[...]) * (1.0 / GLA_TAU)
    la = jnp.where(valid, la, 0.0)
    gq = load(gq_ref) * (GLA_DK ** -0.5)
    gk = jnp.where(valid, load(gk_ref), 0.0)
    gv = jnp.where(valid, load(gv_ref), 0.0)
    ex = _dot_exact_lhs(mats, la)
    b = ex[:c]
    ghead = [(lane128 >= GLA_DK * h) & (lane128 < GLA_DK * (h + 1)) for h in range(GLA_HEADS)]
    vhead = [(lane256 >= GLA_DV * h) & (lane256 < GLA_DV * (h + 1)) for h in range(GLA_HEADS)]
    tmod = lax.broadcasted_iota(jnp.int32, (GLA_HEADS * c, c), 0) % c
    smod = lax.broadcasted_iota(jnp.int32, (GLA_HEADS * c, c), 1)

    def expand(x):
        return jnp.concatenate([jnp.where(ghead[h], x, 0.0) for h in range(GLA_HEADS)], axis=0)

    a_all = jnp.where(tmod == smod, mm(expand(gq), gk, NT_DIMS), 0.0)
    for lv in range(N_LEVELS):
        if (c >> (lv + 1)) >= rows:
            continue
        e = jnp.exp(ex[(lv + 1) * c:(lv + 2) * c])
        bit = N_LEVELS - 1 - lv
        second = ((row_i >> bit) & 1) == 1
        qt = jnp.where(second, gq * e, 0.0)
        kt = jnp.where(second, 0.0, gk * e)
        a_lv = mm(expand(qt), kt, NT_DIMS)
        a_all = a_all + jnp.where((tmod >> (bit + 1)) == (smod >> (bit + 1)), a_lv, 0.0)
    o = jnp.zeros((c, 256), F32)
    for h in range(GLA_HEADS):
        o = o + jnp.where(vhead[h], mm(a_all[h * c:(h + 1) * c], gv, NN_DIMS), 0.0)
    st = st_ref[0]
    o = o + mm(gq * jnp.exp(b), st, NT_DIMS)
    bl = b[c - 1:c]
    kl = gk * jnp.exp(bl - b)
    bd_gla = ((lax.broadcasted_iota(jnp.int32, (256, LANE), 0) // GLA_DV)
              == (lax.broadcasted_iota(jnp.int32, (256, LANE), 1) // GLA_DK))
    st_ref[0] = st * jnp.exp(bl) + jnp.where(bd_gla, mm(gv, kl, TN_DIMS), 0.0)
    ms2 = jnp.zeros((c, 256), F32)
    for h in range(GLA_HEADS):
        s2 = jnp.sum(jnp.where(vhead[h], o * o, 0.0), axis=-1, keepdims=True) * (1.0 / GLA_DV)
        ms2 = ms2 + jnp.where(vhead[h], lax.rsqrt(s2 + RMS_EPS), 0.0)
    ogla = o * ms2 * gnorm_ref[...] * jax.nn.silu(load(gr_ref))
    ogla_ref[...] = ogla[:rows]

    mu = jnp.where(valid, load(mu_ref), 0.0)
    up = jnp.concatenate([prev_ref[...], mu], axis=0)
    cw = cw_ref[...]
    uc = cb_ref[...]
    for i in range(CONV_W):
        off = 8 - (CONV_W - 1) + i
        uc = uc + up[off:off + c] * cw[i:i + 1]
    uc = jax.nn.silu(uc)
    if rows == c:
        prev_ref[...] = mu[c - 8:]
    mq = mm(uc, wq_ref[...], NN_DIMS)
    mk = mm(uc, wk_ref[...], NN_DIMS) * (ML_DH ** -0.5)
    mv = load(mv_ref)
    g_raw = misc + gbias_ref[...]
    is_i = (lane128 >= MS_MI) & (lane128 < MS_MI + ML_HEADS)
    is_f = (lane128 >= MS_MF) & (lane128 < MS_MF + ML_HEADS)
    g = jnp.where(is_i, jnp.where(valid, g_raw, NEG_BIG),
                  jnp.where(is_f, jnp.where(valid, jax.nn.log_sigmoid(g_raw), 0.0), 0.0))
    fc_all = _dot_exact_lhs(tri, jnp.where(is_f, g, 0.0))
    gt = g.T
    ft = fc_all.T
    ms_old = ms_ref[0]
    ns_old = ns_ref[0][0:1]
    ct = ct_ref[0]
    inter = mm(mq, ct, NT_DIMS)
    qn = mq * ns_old
    hc = jnp.zeros((c, 256), F32)
    ws_full = jnp.zeros((c, 256), F32)
    wl_row = jnp.zeros((1, 256), F32)
    ms_new = jnp.zeros((1, LANE), F32)
    for h in range(ML_HEADS):
        fcol = fc_all[:, MS_MF + h:MS_MF + h + 1]
        frow = ft[MS_MF + h:MS_MF + h + 1]
        igrow = gt[MS_MI + h:MS_MI + h + 1]
        igcol = g[:, MS_MI + h:MS_MI + h + 1]
        m0 = ms_old[0:1, h:h + 1]
        d = jnp.where(causal, fcol - frow + igrow, NEG_BIG)
        a = fcol + m0
        m = jnp.maximum(a, jnp.max(d, axis=-1, keepdims=True))
        wp = jnp.exp(a - m)
        w = jnp.exp(d - m)
        qk = mm(jnp.where(vhead[h], mq, 0.0), mk, NT_DIMS) * w
        num = wp * inter + mm(qk, mv, NN_DIMS)
        den = wp * jnp.sum(jnp.where(vhead[h], qn, 0.0), axis=-1, keepdims=True) + jnp.sum(qk, axis=-1, keepdims=True)
        hh = num / jnp.maximum(jnp.abs(den), jnp.exp(-m))
        hc = hc + jnp.where(vhead[h], hh, 0.0)
        m_last = m[c - 1:c]
        f_last = fcol[c - 1:c]
        wl = jnp.exp(f_last + m0 - m_last)
        ws = jnp.exp(f_last - fcol + igcol - m_last)
        ws_full = ws_full + jnp.where(vhead[h], ws, 0.0)
        wl_row = wl_row + jnp.where(vhead[h], wl, 0.0)
        ms_new = ms_new + jnp.where(lane128 == h, m_last, 0.0)
    kw = mk * ws_full
    bd_ml = ((lax.broadcasted_iota(jnp.int32, (256, 256), 0) // ML_DH)
             == (lax.broadcasted_iota(jnp.int32, (256, 256), 1) // ML_DH))
    ct_ref[0] = ct * wl_row + jnp.where(bd_ml, mm(mv, kw, TN_DIMS), 0.0)
    ns_new = wl_row * ns_old + jnp.sum(kw, axis=0, keepdims=True)
    ns_ref[0] = jnp.broadcast_to(ns_new, (8, 256))
    ms_ref[0] = jnp.broadcast_to(ms_new, (8, LANE))
    hc = hc * jax.nn.sigmoid(load(mo_ref))
    hn = jnp.zeros((c, 256), F32)
    for h in range(ML_HEADS):
        mean = jnp.sum(jnp.where(vhead[h], hc, 0.0), axis=-1, keepdims=True) * (1.0 / ML_DH)
        xc = jnp.where(vhead[h], hc - mean, 0.0)
        var = jnp.sum(xc * xc, axis=-1, keepdims=True) * (1.0 / ML_DH)
        hn = hn + xc * lax.rsqrt(var + LN_EPS)
    oml = hn * mnorm_ref[...] + mskip_ref[...] * uc
    oml_ref[...] = oml[:rows]


def _rec(proj, lw, consts, n_seq, n_blocks, rows, lo_row, hi_row, past, precise):
    n = proj.shape[0]
    has_past = past is not None
    rmap = lambda cidx: (lambda s, j: (s * n_blocks + j, cidx))
    in_specs = [pl.BlockSpec((rows, LANE), rmap(PJ_GQ)), pl.BlockSpec((rows, LANE), rmap(PJ_GK)),
                pl.BlockSpec((rows, 256), rmap(PJ_GV)), pl.BlockSpec((rows, 256), rmap(PJ_GR)),
                pl.BlockSpec((rows, 256), rmap(PJ_MU)), pl.BlockSpec((rows, 256), rmap(PJ_MV)),
                pl.BlockSpec((rows, 256), rmap(PJ_MO)), pl.BlockSpec((rows, LANE), rmap(PJ_MISC))]
    args = [proj] * 8
    weights = [consts["mats"], lw["wa"], lw["ba"], lw["gnorm"], lw["conv_w"], lw["conv_b"], lw["wq_bd"], lw["wk_bd"],
               lw["gbias"], lw["mnorm"], lw["mskip"]]
    in_specs += [_const_spec(w.shape) for w in weights]
    args += weights
    smap = lambda s, j: (s, 0, 0)
    state_shapes = [(n_seq, 256, LANE), (n_seq, 256, 256), (n_seq, 8, 256), (n_seq, 8, LANE)]
    if has_past:
        in_specs += [pl.BlockSpec((1,) + sh[1:], smap) for sh in state_shapes] + [pl.BlockSpec((1, 8, 256), smap)]
        args += list(past)
    out_specs = [pl.BlockSpec((rows, 256), rmap(0)), pl.BlockSpec((rows, 256), rmap(0))]
    out_specs += [pl.BlockSpec((1,) + sh[1:], smap) for sh in state_shapes]
    out_shape = [jax.ShapeDtypeStruct((n, 256), F32), jax.ShapeDtypeStruct((n, 256), F32)]
    out_shape += [jax.ShapeDtypeStruct(sh, F32) for sh in state_shapes]
    return pl.pallas_call(
        functools.partial(_rec_kernel, rows=rows, lo_row=lo_row, hi_row=hi_row, has_past=has_past, precise=precise),
        grid=(n_seq, n_blocks),
        in_specs=in_specs,
        out_specs=out_specs,
        out_shape=out_shape,
        scratch_shapes=[pltpu.VMEM((8, 256), F32)],
        compiler_params=_params(("parallel", "arbitrary")),
        name="rec",
    )(*args)


def _post_kernel(x_ref, olat_ref, ogla_ref, oml_ref, wo_ref, wo2_ref, l1g_ref, l1b_ref, rw_ref, rb_ref,
                 wgu_ref, wd_ref, l2g_ref, l2b_ref, o_ref):
    tm = x_ref.shape[0]
    gm = jnp.concatenate([ogla_ref[...], oml_ref[...]], axis=1)
    gb = gm.astype(BF16)
    if wo2_ref.shape[0] == 3 * gm.shape[1]:
        gb = jnp.concatenate([gb, (gm - gb.astype(F32)).astype(BF16), gb], axis=1)
    y = _dot(olat_ref[...], wo_ref[...]) + _dot(gb, wo2_ref[...])
    x1 = _layer_norm(ALPHA * x_ref[...] + y, l1g_ref[...], l1b_ref[...])
    xb = x1.astype(BF16)

    xr = xb
    if rw_ref.shape[0] == 3 * xb.shape[1]:
        xr = jnp.concatenate([xb, (x1 - xb.astype(F32)).astype(BF16), xb], axis=1)
    lg = _dot(xr, rw_ref[...])
    sc = jax.nn.sigmoid(lg).T[:N_EXPERTS]
    sel = sc + rb_ref[...]
    ng, ne = N_GROUPS, E_PER_GROUP
    a_sel = [sel[ng * i:ng * (i + 1)] for i in range(ne)]
    top2 = None
    for i in range(ne):
        for k in range(i + 1, ne):
            ps = a_sel[i] + a_sel[k]
            top2 = ps if top2 is None else jnp.maximum(top2, ps)
    tg = [top2[g:g + 1] for g in range(ng)]
    best = functools.reduce(jnp.maximum, tg)
    gsel, taken = [], jnp.zeros_like(best)
    for g in range(ng):
        hit = jnp.where(tg[g] >= best, 1.0, 0.0) * (1.0 - taken)
        gsel.append(hit)
        taken = taken + hit
    cs = [sum(gsel[g] * jnp.where(gsel[g] > 0.0, sel[ng * i + g:ng * i + g + 1], 0.0) for g in range(ng))
          for i in range(ne)]
    ss = [sum(gsel[g] * jnp.where(gsel[g] > 0.0, sc[ng * i + g:ng * i + g + 1], 0.0) for g in range(ng))
          for i in range(ne)]
    gates = []
    for i in range(ne):
        rank = jnp.zeros_like(best)
        for k in range(ne):
            if k < i:
                rank = rank + jnp.where(cs[k] >= cs[i], 1.0, 0.0)
            elif k > i:
                rank = rank + jnp.where(cs[k] > cs[i], 1.0, 0.0)
        gates.append(jnp.where(rank < 1.5, ss[i], 0.0))
    inv = 1.0 / sum(gates)
    rows = [gates[i] * inv * gsel[g] for i in range(ne) for g in range(ng)]
    wt = jnp.concatenate(rows + [jnp.zeros((LANE - N_EXPERTS, tm), F32)], axis=0)
    wcol = wt.T

    acc = jnp.zeros((tm, x_ref.shape[1]), F32)
    for e in range(N_EXPERTS):
        col = ng * (e % ne) + e // ne
        gu = _dot(xb, wgu_ref[e])
        hh = jax.nn.silu(gu[:, :D_EXPERT]) * gu[:, D_EXPERT:] * wcol[:, col:col + 1]
        acc = acc + _dot(hh.astype(BF16), wd_ref[e])
    o_ref[...] = _layer_norm(ALPHA * x1 + acc, l2g_ref[...], l2b_ref[...])


def _post(x, olat, ogla, oml, lw, rw, rb, tm, precise):
    n, d = x.shape
    row = lambda i: (i, 0)
    single = dict(pipeline_mode=pl.Buffered(1))
    rw = rw[0] if precise else rw[1]
    consts = [lw["wo"], lw["wo2_3"] if precise else lw["wo2"], lw["ln1_g"], lw["ln1_b"], rw, rb, lw["wgu"], lw["wd"], lw["ln2_g"], lw["ln2_b"]]
    in_specs = [pl.BlockSpec((tm, d), row), pl.BlockSpec((tm, 1024), row), pl.BlockSpec((tm, 256), row),
                pl.BlockSpec((tm, 256), row)]
    for w in consts:
        nd = w.ndim
        in_specs.append(pl.BlockSpec(w.shape, (lambda i, _nd=nd: (0,) * _nd), **single))
    return pl.pallas_call(
        _post_kernel,
        grid=(n // tm,),
        in_specs=in_specs,
        out_specs=pl.BlockSpec((tm, d), row),
        out_shape=jax.ShapeDtypeStruct((n, d), F32),
        compiler_params=_params(("parallel",)),
        name="post",
    )(x, olat, ogla, oml, *consts)


def _rope_tables(pos):
    half = MLA_ROPE // 2
    inv = ROPE_THETA ** (-jnp.arange(half, dtype=F32) / half)
    ang = pos.astype(F32)[:, None] * inv
    cos, sin = jnp.cos(ang), jnp.sin(ang)
    c2 = jnp.concatenate([cos, cos], axis=1)
    s2 = jnp.concatenate([-sin, sin], axis=1)
    cq = jnp.tile(c2, (1, MLA_HEADS))
    sq = jnp.tile(s2, (1, MLA_HEADS))
    kt = jnp.concatenate([c2, s2, jnp.zeros((pos.shape[0], LANE - 2 * MLA_ROPE), F32)], axis=1)
    return cq, sq, kt


def _prep_layer(l, p):
    d = p["w_in"].shape[1]
    offs = np.concatenate([[0], np.cumsum(IN_SIZES)])
    wi = p["w_in"][l]
    cq, ckv, kr, gq, gk, gv, ga, gr, mu, mv, mi, mf, mo = [wi[:, offs[i]:offs[i + 1]] for i in range(13)]
    half = MLA_ROPE // 2
    kr_sw = jnp.concatenate([kr[:, half:], kr[:, :half]], axis=1)
    misc = jnp.concatenate([kr, kr_sw, ga, mi, mf, jnp.zeros((d, LANE - MS_MF - ML_HEADS), F32)], axis=1)
    w_in32 = jnp.concatenate([cq, gv, gr, mu, mv, mo, ckv, gq, gk, misc], axis=1)
    w_in = w_in32.astype(BF16)
    w_in3 = _split_weight(w_in32) if l == 0 else None

    dq = MLA_NOPE + MLA_ROPE
    wuq = p["mla_w_uq"][l].reshape(MLA_Q_RANK, MLA_HEADS, dq)
    wukv = p["mla_w_ukv"][l].reshape(MLA_KV_RANK, MLA_HEADS, MLA_NOPE + MLA_V)
    wuk_t = jnp.transpose(wukv[..., :MLA_NOPE], (1, 2, 0))
    wql = _fold(jnp.transpose(wuq[..., :MLA_NOPE], (1, 0, 2)), wuk_t)
    wql = jnp.transpose(wql, (1, 0, 2)).reshape(MLA_Q_RANK, MLA_HEADS * MLA_KV_RANK)
    wqr = wuq[..., MLA_NOPE:]
    wqr_sw = jnp.concatenate([wqr[..., half:], wqr[..., :half]], axis=-1)
    wq = jnp.concatenate([wql, wqr.reshape(MLA_Q_RANK, -1), wqr_sw.reshape(MLA_Q_RANK, -1)], axis=1).astype(BF16)

    wout = p["w_out"][l]
    nv = MLA_HEADS * MLA_V
    wuv = jnp.transpose(wukv[..., MLA_NOPE:], (1, 0, 2))
    wo_mla = _fold(wuv, wout[:nv].reshape(MLA_HEADS, MLA_V, d)).reshape(MLA_HEADS * MLA_KV_RANK, d)
    wo = wo_mla.astype(BF16)
    wo2 = wout[nv:].astype(BF16)
    wo2_3 = _split_weight(wout[nv:]) if l == 0 else None

    wa = jnp.zeros((LANE, LANE), F32).at[MS_GA:MS_GA + GLA_GATE_RANK].set(p["gla_w_a"][l])
    gbias = (jnp.zeros((1, LANE), F32).at[0, MS_MI:MS_MI + ML_HEADS].set(p["ml_b_i"][l])
             .at[0, MS_MF:MS_MF + ML_HEADS].set(p["ml_b_f"][l]))

    def block_diag(w):
        out = jnp.zeros((ML_WIDTH, ML_WIDTH), F32)
        for h in range(ML_HEADS):
            out = out.at[h * ML_DH:(h + 1) * ML_DH, h * ML_DH:(h + 1) * ML_DH].set(w[h])
        return out

    conv_w = jnp.concatenate([p["ml_conv_w"][l], jnp.zeros((8 - CONV_W, ML_WIDTH), F32)], axis=0)
    wgu = jnp.concatenate([p["moe_w_gate"][l], p["moe_w_up"][l]], axis=2).astype(BF16)
    return dict(
        w_in=w_in, w_in3=w_in3, wq=wq, wo=wo, wo2=wo2, wo2_3=wo2_3, wa=wa, gbias=gbias,
        qn=p["mla_q_norm"][l][None], kvn=p["mla_kv_norm"][l][None],
        ba=p["gla_b_a"][l][None], gnorm=jnp.tile(p["gla_norm"][l], GLA_HEADS)[None],
        conv_w=conv_w, conv_b=p["ml_conv_b"][l][None], wq_bd=block_diag(p["ml_w_q"][l]),
        wk_bd=block_diag(p["ml_w_k"][l]), mnorm=p["ml_norm"][l][None], mskip=p["ml_skip"][l][None],
        ln1_g=p["ln1_g"][l][None], ln1_b=p["ln1_b"][l][None], ln2_g=p["ln2_g"][l][None], ln2_b=p["ln2_b"][l][None],
        wgu=wgu, wd=p["moe_w_down"][l].astype(BF16),
    )


def _gla_state_out(st):
    n = st.shape[0]
    s5 = st.reshape(n, GLA_HEADS, GLA_DV, GLA_HEADS, GLA_DK)
    return jnp.stack([jnp.transpose(s5[:, h, :, h, :], (0, 2, 1)) for h in range(GLA_HEADS)], axis=1)


def _gla_state_in(s):
    n = s.shape[0]
    out = jnp.zeros((n, GLA_HEADS, GLA_DV, GLA_HEADS, GLA_DK), F32)
    for h in range(GLA_HEADS):
        out = out.at[:, h, :, h, :].set(jnp.transpose(s[:, h], (0, 2, 1)))
    return out.reshape(n, GLA_HEADS * GLA_DV, GLA_HEADS * GLA_DK)


def _ml_state_out(ct):
    n = ct.shape[0]
    c5 = ct.reshape(n, ML_HEADS, ML_DH, ML_HEADS, ML_DH)
    return jnp.stack([jnp.transpose(c5[:, h, :, h, :], (0, 2, 1)) for h in range(ML_HEADS)], axis=1)


def _ml_state_in(cs):
    n = cs.shape[0]
    out = jnp.zeros((n, ML_HEADS, ML_DH, ML_HEADS, ML_DH), F32)
    for h in range(ML_HEADS):
        out = out.at[:, h, :, h, :].set(jnp.transpose(cs[:, h], (0, 2, 1)))
    return out.reshape(n, ML_WIDTH, ML_WIDTH)


def _pick_tile(n, cands):
    for t in cands:
        if n % t == 0:
            return t
    raise ValueError(f"no row tile for {n} rows")


def kernel(x_prompt, x_sample, cache_kv_latent, cache_k_rope, state_gla, state_mlstm_C, state_mlstm_n, state_mlstm_m, state_mlstm_conv, page_table, meta_tokens, ln_in_g, ln_in_b, w_in, mla_q_norm, mla_w_uq, mla_kv_norm, mla_w_ukv, gla_w_a, gla_b_a, gla_norm, ml_conv_w, ml_conv_b, ml_w_q, ml_w_k, ml_b_i, ml_b_f, ml_norm, ml_skip, w_out, ln1_g, ln1_b, ln2_g, ln2_b, router_w, router_bias, moe_w_gate, moe_w_up, moe_w_down):
    p = dict(w_in=w_in, mla_q_norm=mla_q_norm, mla_w_uq=mla_w_uq, mla_kv_norm=mla_kv_norm, mla_w_ukv=mla_w_ukv,
             gla_w_a=gla_w_a, gla_b_a=gla_b_a, gla_norm=gla_norm, ml_conv_w=ml_conv_w, ml_conv_b=ml_conv_b,
             ml_w_q=ml_w_q, ml_w_k=ml_w_k, ml_b_i=ml_b_i, ml_b_f=ml_b_f, ml_norm=ml_norm, ml_skip=ml_skip,
             w_out=w_out, ln1_g=ln1_g, ln1_b=ln1_b, ln2_g=ln2_g, ln2_b=ln2_b, moe_w_gate=moe_w_gate,
             moe_w_up=moe_w_up, moe_w_down=moe_w_down)
    depth = w_in.shape[0]
    nb, seq, d = x_prompt.shape
    ns, t_new, _ = x_sample.shape
    n_pages = page_table.shape[1]
    page = cache_kv_latent.shape[2]
    past_len = n_pages * page
    lp = PAD + seq
    assert seq % CHUNK == 0 and t_new == 8 and t_new >= CONV_W - 1

    layers = [_prep_layer(l, p) for l in range(depth)]
    consts = dict(mats=jnp.asarray(_level_mats(), BF16))
    perm = np.array([E_PER_GROUP * g + i for i in range(E_PER_GROUP) for g in range(N_GROUPS)])
    rw_hi, rw_mid = _split2_bits(router_w.astype(F32)[:, perm])
    lane_pad = ((0, 0), (0, LANE - N_EXPERTS))
    rw = (jnp.pad(jnp.concatenate([rw_hi, rw_hi, rw_mid], axis=0), lane_pad),
          jnp.pad(router_w.astype(F32)[:, perm].astype(BF16), lane_pad))
    rb = router_bias.astype(F32)[perm][:, None]
    g_in, b_in = ln_in_g[None], ln_in_b[None]

    meta = jnp.broadcast_to(meta_tokens[None].astype(F32), (nb, N_META, d))
    xp = jnp.concatenate([jnp.zeros((nb, PAD - N_META, d), F32), meta, x_prompt], axis=1).reshape(nb * lp, d)
    tabs_p = _rope_tables(jnp.arange(lp) - (PAD - N_META))
    tm_p = _pick_tile(nb * lp, (512, 384, 256, 128))
    tk_p = _pick_tile(lp, (384, 128))
    p_out = dict(kv=[], kr=[], gla=[], C=[], n=[], m=[], conv=[])
    for l in range(depth):
        lw = layers[l]
        precise = l == 0
        proj, xp = _in_proj(xp, g_in, b_in, lw["w_in3"] if precise else lw["w_in"], l == 0, tm_p)
        qlat, qrope, ckvn, krope, kcat, vt = _mla_prep(proj, tabs_p, lw["qn"], lw["kvn"], lw["wq"], tk_p, lp // tk_p,
                                                       MLA_SCALE * LOG2E)
        olat = _attn_prompt(qlat, qrope, kcat, vt, nb, lp, CHUNK, tk_p)
        ogla, oml, st, ct, nst, mst = _rec(proj, lw, consts, nb, lp // CHUNK, CHUNK, PAD - N_META, lp, None, precise)
        xp = _post(xp, olat, ogla, oml, lw, rw, rb, tm_p, precise)
        lo = PAD - N_META
        p_out["kv"].append(ckvn.reshape(nb, lp, -1)[:, lo:])
        p_out["kr"].append(krope.reshape(nb, lp, -1)[:, lo:])
        p_out["gla"].append(_gla_state_out(st))
        p_out["C"].append(_ml_state_out(ct))
        p_out["n"].append(nst[:, 0].reshape(nb, ML_HEADS, ML_DH))
        p_out["m"].append(mst[:, 0, :ML_HEADS])
        mu_cols = proj[:, PJ_MU * 256:(PJ_MU + 1) * 256].reshape(nb, lp, ML_WIDTH)
        p_out["conv"].append(mu_cols[:, lp - (CONV_W - 1):])
    y_prompt = xp.reshape(nb, lp, d)[:, PAD:]

    xs = x_sample.reshape(ns * t_new, d)
    tabs_s = _rope_tables(past_len + jnp.arange(t_new))
    tm_s = _pick_tile(ns * t_new, (512, 256, 128, 64, 32, 16))
    tabs_s = tuple(jnp.tile(t, (tm_s // t_new, 1)) for t in tabs_s)
    cache_rt = jnp.swapaxes(cache_k_rope, 2, 3)
    s_out = dict(kv=[], kr=[], gla=[], C=[], n=[], m=[], conv=[])
    for l in range(depth):
        lw = layers[l]
        precise = l == 0
        proj, xs = _in_proj(xs, g_in, b_in, lw["w_in3"] if precise else lw["w_in"], l == 0, tm_s)
        qlat, qrope, ckvn, krope, kcat, _ = _mla_prep(proj, tabs_s, lw["qn"], lw["kvn"], lw["wq"], tm_s, 1, MLA_SCALE)
        olat = _attn_sample(qlat, qrope, kcat, cache_kv_latent, cache_rt, page_table, t_new, l)
        ns0 = jnp.broadcast_to(state_mlstm_n[l].reshape(ns, 1, ML_WIDTH), (ns, 8, ML_WIDTH))
        ms0 = jnp.broadcast_to(jnp.pad(state_mlstm_m[l], ((0, 0), (0, LANE - ML_HEADS)))[:, None], (ns, 8, LANE))
        cv0 = jnp.pad(state_mlstm_conv[l], ((0, 0), (8 - (CONV_W - 1), 0), (0, 0)))
        past = (_gla_state_in(state_gla[l]), _ml_state_in(state_mlstm_C[l]), ns0, ms0, cv0)
        ogla, oml, st, ct, nst, mst = _rec(proj, lw, consts, ns, 1, t_new, 0, t_new, past, precise)
        xs = _post(xs, olat, ogla, oml, lw, rw, rb, tm_s, precise)
        s_out["kv"].append(ckvn.reshape(ns, t_new, -1))
        s_out["kr"].append(krope.reshape(ns, t_new, -1))
        s_out["gla"].append(_gla_state_out(st))
        s_out["C"].append(_ml_state_out(ct))
        s_out["n"].append(nst[:, 0].reshape(ns, ML_HEADS, ML_DH))
        s_out["m"].append(mst[:, 0, :ML_HEADS])
        mu_cols = proj[:, PJ_MU * 256:(PJ_MU + 1) * 256].reshape(ns, t_new, ML_WIDTH)
        s_out["conv"].append(mu_cols[:, t_new - (CONV_W - 1):])
    y_sample = xs.reshape(ns, t_new, d)

    st_p = [jnp.stack(p_out[k]) for k in ("kv", "kr", "gla", "C", "n", "m", "conv")]
    st_s = [jnp.stack(s_out[k]) for k in ("kv", "kr", "gla", "C", "n", "m", "conv")]
    return (y_prompt, y_sample, *st_p, *st_s)
```
